```python
import jax, jax.numpy as jnp
from jax import lax
import numpy as np

D_MODEL = 2048
BATCH = 4
SEQ = 4096
DEPTH = 1

HEAD_DIM = 128
POOL_WINDOWS = (2, 4, 8, 16)
POOL_GROUPS = 4
POOL_GROUP_DIM = D_MODEL // 16
POOL_WIDTH = POOL_GROUPS * POOL_GROUP_DIM
ATTN_Q_HEADS = 8
ATTN_KV_HEADS = 2
ATTN_GROUP = ATTN_Q_HEADS // ATTN_KV_HEADS
WINDOW = 128
ATTN_BLOCK = 128
MEM_HEADS = 4
MEM_LEN = 256
N_BRANCHES = 3
N_EXPERTS = 32
TOP_K = 4
D_FF = D_MODEL
SWIGLU_LIMIT = 7.0
SWIGLU_ALPHA = 1.702
MOE_BLOCK = 256
ROPE_THETA = 10000.0
NORM_EPS = 1e-6
NEG_INF = -1e30

Q_WIDTH = ATTN_Q_HEADS * HEAD_DIM
KV_WIDTH = ATTN_KV_HEADS * HEAD_DIM
MEMQ_WIDTH = MEM_HEADS * HEAD_DIM
GATE_WIDTH = N_BRANCHES * D_MODEL
IN_WIDTH = POOL_WIDTH + Q_WIDTH + 2 * KV_WIDTH + MEMQ_WIDTH + GATE_WIDTH

kernel_name = 'hybrid_pool_localgqa_memxattn_moe'


def rms_norm(x, gain):
    xf = x.astype(jnp.float32)
    y = xf * lax.rsqrt(jnp.mean(xf * xf, axis=-1, keepdims=True) + NORM_EPS)
    return (y * gain.astype(jnp.float32)).astype(x.dtype)


def rope(x, positions):
    half = HEAD_DIM // 2
    inv_freq = jnp.power(jnp.float32(ROPE_THETA), -jnp.arange(half, dtype=jnp.float32) * (2.0 / HEAD_DIM))
    ang = positions.astype(jnp.float32)[..., None] * inv_freq
    cos = jnp.cos(ang)[:, :, None, :]
    sin = jnp.sin(ang)[:, :, None, :]
    xf = x.astype(jnp.float32)
    x1, x2 = xf[..., :half], xf[..., half:]
    return jnp.concatenate([x1 * cos - x2 * sin, x2 * cos + x1 * sin], axis=-1).astype(x.dtype)


def multiscale_pool(u):
    bsz, seq, _ = u.shape
    ug = u.reshape(bsz, seq, POOL_GROUPS, POOL_GROUP_DIM).astype(jnp.float32)
    cs = jnp.concatenate([jnp.zeros((bsz, 1, POOL_GROUPS, POOL_GROUP_DIM), jnp.float32),
                          jnp.cumsum(ug, axis=1)], axis=1)
    pos = jnp.arange(seq, dtype=jnp.int32)
    means = []
    for gi, w in enumerate(POOL_WINDOWS):
        lo = jnp.clip(pos - w // 2, 0, seq)
        hi = jnp.clip(pos - w // 2 + w, 0, seq)
        win_sum = cs[:, hi, gi] - cs[:, lo, gi]
        cnt = (hi - lo).astype(jnp.float32)
        means.append(win_sum / cnt[None, :, None])
    pooled = jnp.stack(means, axis=2)
    return (pooled - ug).astype(u.dtype)


def local_gqa(q, k, v, sink):
    bsz, seq = q.shape[0], q.shape[1]
    nb = seq // ATTN_BLOCK
    qb = q.reshape(bsz, nb, ATTN_BLOCK, ATTN_KV_HEADS, ATTN_GROUP, HEAD_DIM)

    def band(t):
        tb = t.reshape(bsz, nb, ATTN_BLOCK, ATTN_KV_HEADS, HEAD_DIM)
        tp = jnp.pad(tb, ((0, 0), (1, 1), (0, 0), (0, 0), (0, 0)))
        return jnp.concatenate([tp[:, :-2], tp[:, 1:-1], tp[:, 2:]], axis=2)

    kw, vw = band(k), band(v)
    scale = 1.0 / np.sqrt(HEAD_DIM)
    s = jnp.einsum('bnqkgd,bnjkd->bnkgqj', qb, kw).astype(jnp.float32) * scale
    qi = jnp.arange(ATTN_BLOCK)[:, None] + ATTN_BLOCK
    kj = jnp.arange(3 * ATTN_BLOCK)[None, :]
    in_band = jnp.abs(kj - qi) <= WINDOW
    key_abs = (jnp.arange(nb)[:, None] - 1) * ATTN_BLOCK + jnp.arange(3 * ATTN_BLOCK)[None, :]
    valid = (key_abs >= 0) & (key_abs < seq)
    mask = in_band[None, :, :] & valid[:, None, :]
    s = jnp.where(mask[None, :, None, None], s, NEG_INF)
    sk = sink.astype(jnp.float32).reshape(ATTN_KV_HEADS, ATTN_GROUP)[None, None, :, :, None, None]
    m = jnp.maximum(jnp.max(s, axis=-1, keepdims=True), sk)
    p = jnp.exp(s - m)
    denom = jnp.sum(p, axis=-1, keepdims=True) + jnp.exp(sk - m)
    probs = (p / denom).astype(v.dtype)
    o = jnp.einsum('bnkgqj,bnjkd->bnqkgd', probs, vw)
    return o.reshape(bsz, seq, ATTN_Q_HEADS * HEAD_DIM)


def memory_attention(q, k, v):
    scale = 1.0 / np.sqrt(HEAD_DIM)
    s = jnp.einsum('bshd,bmhd->bhsm', q, k).astype(jnp.float32) * scale
    probs = jax.nn.softmax(s, axis=-1).astype(v.dtype)
    o = jnp.einsum('bhsm,bmhd->bshd', probs, v)
    return o.reshape(q.shape[0], q.shape[1], MEMQ_WIDTH)


def moe_ffn(xn, l, w_router, b_router, w_up, b_up, w_down, b_down):
    bsz, seq, d = xn.shape
    n_tok = bsz * seq
    n_assign = n_tok * TOP_K
    xt = xn.reshape(n_tok, d)
    logits = jnp.dot(xt, w_router[l]).astype(jnp.float32) + b_router[l].astype(jnp.float32)
    top_logits, top_idx = lax.top_k(logits, TOP_K)
    top_w = jax.nn.softmax(top_logits, axis=-1)
    e_flat = top_idx.reshape(-1).astype(jnp.int32)
    t_flat = jnp.arange(n_assign, dtype=jnp.int32) // TOP_K
    g_flat = top_w.reshape(-1)
    order = jnp.argsort(e_flat, stable=True)
    e_s, t_s, g_s = e_flat[order], t_flat[order], g_flat[order]
    counts = jnp.bincount(e_flat, length=N_EXPERTS).astype(jnp.int32)
    starts = jnp.cumsum(counts) - counts
    padded = (counts + MOE_BLOCK - 1) // MOE_BLOCK * MOE_BLOCK
    pad_ends = jnp.cumsum(padded)
    pad_starts = pad_ends - padded
    dest = pad_starts[e_s] + (jnp.arange(n_assign, dtype=jnp.int32) - starts[e_s])
    n_blocks = -(-n_assign // MOE_BLOCK) + N_EXPERTS
    cap = n_blocks * MOE_BLOCK
    tok_buf = jnp.zeros((cap,), jnp.int32).at[dest].set(t_s)
    gate_buf = jnp.zeros((cap,), jnp.float32).at[dest].set(g_s)
    blk_start = jnp.arange(n_blocks, dtype=jnp.int32) * MOE_BLOCK
    blk_expert = jnp.minimum(jnp.searchsorted(pad_ends, blk_start, side='right'),
                             N_EXPERTS - 1).astype(jnp.int32)

    def expert_block(args):
        tok, e = args
        xb = xt[tok]
        hb = jnp.dot(xb, w_up[l, e]) + b_up[l, e]
        gate = jnp.minimum(hb[:, 0::2], SWIGLU_LIMIT)
        up = jnp.clip(hb[:, 1::2], -SWIGLU_LIMIT, SWIGLU_LIMIT)
        act = gate * jax.nn.sigmoid(SWIGLU_ALPHA * gate) * (up + 1.0)
        return jnp.dot(act, w_down[l, e]) + b_down[l, e]

    y_buf = lax.map(expert_block, (tok_buf.reshape(n_blocks, MOE_BLOCK), blk_expert))
    y = jnp.zeros((n_tok, d), jnp.float32).at[tok_buf].add(
        y_buf.reshape(cap, d).astype(jnp.float32) * gate_buf[:, None])
    return y.reshape(bsz, seq, d).astype(xn.dtype)


def setup_inputs(seed: int = 0) -> dict:
    key = jax.random.key(seed)
    ks = jax.random.split(key, 32)
    f32 = jnp.float32
    L, D = DEPTH, D_MODEL

    def nrm(k, shape, fan_in):
        return jax.random.normal(k, shape, f32) * (fan_in ** -0.5)

    def gain(k, shape):
        return 1.0 + 0.05 * jax.random.normal(k, shape, f32)

    x = jax.random.normal(ks[0], (BATCH, SEQ, D), f32)
    mem = jax.random.normal(ks[1], (BATCH, MEM_LEN, D), f32)
    offsets = jax.random.randint(ks[2], (BATCH, 1), 0, 1024, dtype=jnp.int32)
    positions = offsets + jnp.arange(SEQ, dtype=jnp.int32)[None, :]
    return {
        'x': x,
        'mem': mem,
        'positions': positions,
        'norm1_gain': gain(ks[3], (L, D)),
        'w_in': nrm(ks[4], (L, D, IN_WIDTH), D),
        'b_gate': 0.1 * jax.random.normal(ks[5], (L, GATE_WIDTH), f32),
        'w_pool_group': nrm(ks[6], (L, POOL_GROUPS, POOL_GROUP_DIM, POOL_GROUP_DIM), POOL_GROUP_DIM),
        'pool_scale': gain(ks[7], (L, POOL_WIDTH)),
        'w_pool_proj': nrm(ks[8], (L, POOL_WIDTH, D), POOL_WIDTH),
        'attn_q_norm': gain(ks[9], (L, HEAD_DIM)),
        'attn_k_norm': gain(ks[10], (L, HEAD_DIM)),
        'attn_sink': 0.5 * jax.random.normal(ks[11], (L, ATTN_Q_HEADS), f32),
        'w_attn_proj': nrm(ks[12], (L, Q_WIDTH, D), Q_WIDTH),
        'mem_norm_gain': gain(ks[13], (L, D)),
        'w_mem_kv': nrm(ks[14], (L, D, 2 * MEMQ_WIDTH), D),
        'mem_q_norm': gain(ks[15], (L, HEAD_DIM)),
        'mem_k_norm': gain(ks[16], (L, HEAD_DIM)),
        'w_mem_proj': nrm(ks[17], (L, MEMQ_WIDTH, D), MEMQ_WIDTH),
        'w_out': nrm(ks[18], (L, D, D), D),
        'norm2_gain': gain(ks[19], (L, D)),
        'w_router': nrm(ks[20], (L, D, N_EXPERTS), D),
        'b_router': 0.01 * jax.random.normal(ks[21], (L, N_EXPERTS), f32),
        'w_up': nrm(ks[22], (L, N_EXPERTS, D, 2 * D_FF), D),
        'b_up': 0.02 * jax.random.normal(ks[23], (L, N_EXPERTS, 2 * D_FF), f32),
        'w_down': nrm(ks[24], (L, N_EXPERTS, D_FF, D), D_FF),
        'b_down': 0.02 * jax.random.normal(ks[25], (L, N_EXPERTS, D), f32),
    }


def reference(x, mem, positions, norm1_gain, w_in, b_gate, w_pool_group, pool_scale, w_pool_proj,
              attn_q_norm, attn_k_norm, attn_sink, w_attn_proj, mem_norm_gain, w_mem_kv,
              mem_q_norm, mem_k_norm, w_mem_proj, w_out, norm2_gain, w_router, b_router,
              w_up, b_up, w_down, b_down):
    bsz, seq, d = x.shape
    n_mem = mem.shape[1]
    h = x
    for l in range(DEPTH):
        xn = rms_norm(h, norm1_gain[l])
        proj = jnp.dot(xn, w_in[l])
        o0 = 0
        u_pool = proj[..., o0:o0 + POOL_WIDTH]; o0 += POOL_WIDTH
        u_q = proj[..., o0:o0 + Q_WIDTH]; o0 += Q_WIDTH
        u_k = proj[..., o0:o0 + KV_WIDTH]; o0 += KV_WIDTH
        u_v = proj[..., o0:o0 + KV_WIDTH]; o0 += KV_WIDTH
        u_mq = proj[..., o0:o0 + MEMQ_WIDTH]; o0 += MEMQ_WIDTH
        u_gate = proj[..., o0:o0 + GATE_WIDTH]

        pooled = multiscale_pool(u_pool)
        a = jnp.einsum('bsgc,gcd->bsgd', pooled, w_pool_group[l]).reshape(bsz, seq, POOL_WIDTH)
        br_a = jnp.dot(a * pool_scale[l], w_pool_proj[l])

        q = u_q.reshape(bsz, seq, ATTN_Q_HEADS, HEAD_DIM)
        k = u_k.reshape(bsz, seq, ATTN_KV_HEADS, HEAD_DIM)
        v = u_v.reshape(bsz, seq, ATTN_KV_HEADS, HEAD_DIM)
        q = rope(rms_norm(q, attn_q_norm[l]), positions)
        k = rope(rms_norm(k, attn_k_norm[l]), positions)
        br_b = jnp.dot(local_gqa(q, k, v, attn_sink[l]), w_attn_proj[l])

        mn = rms_norm(mem, mem_norm_gain[l])
        kv = jnp.dot(mn, w_mem_kv[l]).reshape(bsz, n_mem, 2, MEM_HEADS, HEAD_DIM)
        mk = rms_norm(kv[:, :, 0], mem_k_norm[l])
        mv = kv[:, :, 1]
        mq = rms_norm(u_mq.reshape(bsz, seq, MEM_HEADS, HEAD_DIM), mem_q_norm[l])
        br_c = jnp.dot(memory_attention(mq, mk, mv), w_mem_proj[l])

        g = jax.nn.sigmoid((u_gate + b_gate[l]).astype(jnp.float32)).astype(h.dtype)
        g = g.reshape(bsz, seq, N_BRANCHES, d)
        merged = g[:, :, 0] * br_a + g[:, :, 1] * br_b + g[:, :, 2] * br_c
        h = h + jnp.dot(merged, w_out[l])

        h = h + moe_ffn(rms_norm(h, norm2_gain[l]), l, w_router, b_router, w_up, b_up, w_down, b_down)
    return h
```

```python
import functools

import numpy as np
import jax
import jax.numpy as jnp
from jax import lax
from jax.experimental import pallas as pl
from jax.experimental.pallas import tpu as pltpu

f32 = jnp.float32
bf16 = jnp.bfloat16
i32 = jnp.int32
u32 = jnp.uint32

D = 2048
HD = 128
POOL_WINDOWS = (2, 4, 8, 16)
POOL_W = 512
QW = 1024
KVW = 256
MQW = 512
A_WIDTH = POOL_W + QW + 2 * KVW + MQW
N_EXP = 32
TOP_K = 4
D_FF = 2048
LIMIT = 7.0
ALPHA = 1.702
MOE_BLOCK = 256
EPS = 1e-6
NEG = -1e30
THETA = 10000.0
WINDOW = 128

LANES = 128
SLAB = 8
HALF = D // 2
VMEM_LIMIT = 56 * 1024 * 1024

TM_IN = 512
TM_MG = 512
TN_MG = 256
TR = 512
TD = 256
TC = 128
TN_UP = 1024
POOL_CHUNK = 512
HALO = 8


def _cparams(sem):
    return pltpu.CompilerParams(dimension_semantics=sem, vmem_limit_bytes=VMEM_LIMIT)


def _dot(a, b):
    return jnp.dot(a, b, preferred_element_type=f32)


def _dot_nt(a, b):
    return lax.dot_general(a, b, (((1,), (1,)), ((), ())), preferred_element_type=f32)


def _rms(x, gain):
    return x * lax.rsqrt(jnp.mean(x * x, axis=-1, keepdims=True) + EPS) * gain


def _bits(x_f32):
    return lax.bitcast_convert_type(x_f32, u32)


def _unbits(x_u32):
    return lax.bitcast_convert_type(x_u32, f32)


def _pack_store(dst_ref, vals_bf16, rows):
    bits = _bits(vals_bf16.astype(f32))
    for c in range(SLAB):
        lo = bits[:, c * LANES:(c + 1) * LANES] >> 16
        hi = bits[:, HALF + c * LANES:HALF + (c + 1) * LANES] & jnp.uint32(0xFFFF0000)
        dst_ref[pl.ds(c, rows, stride=SLAB), :] = hi | lo


def _unpack_chunk(words):
    lo = _unbits(words << 16)
    hi = _unbits(words & jnp.uint32(0xFFFF0000))
    return lo, hi


def _inproj_kernel(x_ref, g1_ref, w_ref, pos_ref, invf_ref, qg_ref, kg_ref, mg_ref,
                   pool_ref, q_ref, k_ref, v_ref, mq_ref):
    x = x_ref[...]
    xn = _rms(x, g1_ref[...]).astype(bf16)
    ang = pos_ref[...].astype(f32) * invf_ref[...]
    cos = jnp.cos(ang)
    sin = jnp.sin(ang)
    lane = lax.broadcasted_iota(i32, ang.shape, 1)
    sin_s = jnp.where(lane < HD // 2, -sin, sin)

    def rope(u):
        return u * cos + pltpu.roll(u, HD // 2, 1) * sin_s

    pool_ref[...] = _dot(xn, w_ref[:, 0:POOL_W])
    o0 = POOL_W
    uq = _dot(xn, w_ref[:, o0:o0 + QW])
    for h in range(QW // HD):
        q_ref[:, h * HD:(h + 1) * HD] = rope(_rms(uq[:, h * HD:(h + 1) * HD], qg_ref[...])).astype(bf16)
    o0 += QW
    uk = _dot(xn, w_ref[:, o0:o0 + KVW])
    for h in range(KVW // HD):
        k_ref[:, h * HD:(h + 1) * HD] = rope(_rms(uk[:, h * HD:(h + 1) * HD], kg_ref[...])).astype(bf16)
    o0 += KVW
    v_ref[...] = _dot(xn, w_ref[:, o0:o0 + KVW]).astype(bf16)
    o0 += KVW
    um = _dot(xn, w_ref[:, o0:o0 + MQW])
    for h in range(MQW // HD):
        mq_ref[:, h * HD:(h + 1) * HD] = _rms(um[:, h * HD:(h + 1) * HD], mg_ref[...]).astype(bf16)


def _inproj(x2, g1, w_in_bf, pos_col, invf, qg, kg, mg):
    n = x2.shape[0]
    row = lambda i: (i, 0)
    const = lambda i: (0, 0)
    return pl.pallas_call(
        _inproj_kernel,
        grid=(n // TM_IN,),
        in_specs=[
            pl.BlockSpec((TM_IN, D), row),
            pl.BlockSpec((1, D), const),
            pl.BlockSpec((D, A_WIDTH), const),
            pl.BlockSpec((TM_IN, 1), row),
            pl.BlockSpec((1, HD), const),
            pl.BlockSpec((1, HD), const),
            pl.BlockSpec((1, HD), const),
            pl.BlockSpec((1, HD), const),
        ],
        out_specs=[
            pl.BlockSpec((TM_IN, POOL_W), row),
            pl.BlockSpec((TM_IN, QW), row),
            pl.BlockSpec((TM_IN, KVW), row),
            pl.BlockSpec((TM_IN, KVW), row),
            pl.BlockSpec((TM_IN, MQW), row),
        ],
        out_shape=[
            jax.ShapeDtypeStruct((n, POOL_W), f32),
            jax.ShapeDtypeStruct((n, QW), bf16),
            jax.ShapeDtypeStruct((n, KVW), bf16),
            jax.ShapeDtypeStruct((n, KVW), bf16),
            jax.ShapeDtypeStruct((n, MQW), bf16),
        ],
        compiler_params=_cparams(("arbitrary",)),
        name="inproj",
    )(x2, g1, w_in_bf, pos_col, invf, qg, kg, mg)


def _pool_kernel(u_ref, wg_ref, sc_ref, a_ref, pad_ref, *, seq):
    zeros = jnp.zeros((HALO, POOL_W), f32)
    pad_ref[0:HALO, :] = zeros
    pad_ref[seq + HALO:seq + 2 * HALO, :] = zeros
    pad_ref[HALO:seq + HALO, :] = u_ref[0]

    def chunk(ci, carry):
        c0 = pl.multiple_of(ci * POOL_CHUNK, POOL_CHUNK)
        pos = c0 + lax.broadcasted_iota(i32, (POOL_CHUNK, 1), 0)
        for g, w in enumerate(POOL_WINDOWS):
            ext = pad_ref[pl.ds(c0, POOL_CHUNK + 2 * HALO), g * HD:(g + 1) * HD]
            tok = ext[HALO:HALO + POOL_CHUNK]
            win = None
            for o in range(-(w // 2), w // 2):
                term = ext[HALO + o:HALO + o + POOL_CHUNK]
                win = term if win is None else win + term
            lo = jnp.clip(pos - w // 2, 0, seq)
            hi = jnp.clip(pos - w // 2 + w, 0, seq)
            cnt = (hi - lo).astype(f32)
            pooled = (win / cnt - tok).astype(bf16)
            a = _dot(pooled, wg_ref[g]) * sc_ref[:, g * HD:(g + 1) * HD]
            a_ref[0, pl.ds(c0, POOL_CHUNK), g * HD:(g + 1) * HD] = a.astype(bf16)
        return carry

    lax.fori_loop(0, seq // POOL_CHUNK, chunk, 0)


def _pool(u3, wg_bf, scale):
    b, seq, _ = u3.shape
    return pl.pallas_call(
        functools.partial(_pool_kernel, seq=seq),
        grid=(b,),
        in_specs=[
            pl.BlockSpec((1, seq, POOL_W), lambda i: (i, 0, 0)),
            pl.BlockSpec((len(POOL_WINDOWS), HD, HD), lambda i: (0, 0, 0)),
            pl.BlockSpec((1, POOL_W), lambda i: (0, 0)),
        ],
        out_specs=pl.BlockSpec((1, seq, POOL_W), lambda i: (i, 0, 0)),
        out_shape=jax.ShapeDtypeStruct((b, seq, POOL_W), bf16),
        scratch_shapes=[pltpu.VMEM((seq + 2 * HALO, POOL_W), f32)],
        compiler_params=_cparams(("arbitrary",)),
        name="pool",
    )(u3, wg_bf, scale)


def _memkv_kernel(m_ref, g_ref, w_ref, kg_ref, mk_ref, mv_ref):
    mn = _rms(m_ref[...], g_ref[...]).astype(bf16)
    kv = _dot(mn, w_ref[...])
    for h in range(MQW // HD):
        mk_ref[:, h * HD:(h + 1) * HD] = _rms(kv[:, h * HD:(h + 1) * HD], kg_ref[...]).astype(bf16)
    mv_ref[...] = kv[:, MQW:].astype(bf16)


def _memkv(mem2, gain, w_bf, kg, n_mem):
    rows = mem2.shape[0]
    return pl.pallas_call(
        _memkv_kernel,
        grid=(rows // n_mem,),
        in_specs=[
            pl.BlockSpec((n_mem, D), lambda i: (i, 0)),
            pl.BlockSpec((1, D), lambda i: (0, 0)),
            pl.BlockSpec((D, 2 * MQW), lambda i: (0, 0)),
            pl.BlockSpec((1, HD), lambda i: (0, 0)),
        ],
        out_specs=[pl.BlockSpec((n_mem, MQW), lambda i: (i, 0))] * 2,
        out_shape=[jax.ShapeDtypeStruct((rows, MQW), bf16)] * 2,
        compiler_params=_cparams(("arbitrary",)),
        name="memkv",
    )(mem2, gain, w_bf, kg)


def _attn_kernel(sink_ref, q_ref, kp_ref, kc_ref, kn_ref, vp_ref, vc_ref, vn_ref, mq_ref, mk_ref, mv_ref,
                 ao_ref, mo_ref, *, seq):
    blk = WINDOW
    nb = pl.program_id(1)
    scale = 1.0 / np.sqrt(HD)
    group = (QW // HD) // (KVW // HD)
    rows = group * blk

    r = lax.broadcasted_iota(i32, (rows, 3 * blk), 0)
    c = lax.broadcasted_iota(i32, (rows, 3 * blk), 1)
    qi = (r % blk) + blk
    key_abs = (nb - 1) * blk + c
    mask = (jnp.abs(c - qi) <= WINDOW) & (key_abs >= 0) & (key_abs < seq)
    rcol = lax.broadcasted_iota(i32, (rows, 1), 0) // blk

    for kh in range(KVW // HD):
        ks = slice(kh * HD, (kh + 1) * HD)
        kb = jnp.concatenate([kp_ref[:, ks], kc_ref[:, ks], kn_ref[:, ks]], axis=0)
        vb = jnp.concatenate([vp_ref[:, ks], vc_ref[:, ks], vn_ref[:, ks]], axis=0)
        qh = jnp.concatenate(
            [q_ref[:, (kh * group + g) * HD:(kh * group + g + 1) * HD] for g in range(group)], axis=0)
        s = _dot_nt(qh, kb) * scale
        s = jnp.where(mask, s, NEG)
        sk = jnp.zeros((rows, 1), f32)
        for g in range(group):
            sk = jnp.where(rcol == g, sink_ref[kh * group + g], sk)
        m = jnp.maximum(jnp.max(s, axis=-1, keepdims=True), sk)
        p = jnp.exp(s - m)
        denom = jnp.sum(p, axis=-1, keepdims=True) + jnp.exp(sk - m)
        probs = (p / denom).astype(bf16)
        o = _dot(probs, vb)
        for g in range(group):
            h = kh * group + g
            ao_ref[:, h * HD:(h + 1) * HD] = o[g * blk:(g + 1) * blk].astype(bf16)

    for h in range(MQW // HD):
        hs = slice(h * HD, (h + 1) * HD)
        s = _dot_nt(mq_ref[:, hs], mk_ref[:, hs]) * scale
        m = jnp.max(s, axis=-1, keepdims=True)
        p = jnp.exp(s - m)
        probs = (p / jnp.sum(p, axis=-1, keepdims=True)).astype(bf16)
        mo_ref[:, hs] = _dot(probs, mv_ref[:, hs]).astype(bf16)


def _attn(sink, q, k, v, mq, mk, mv, bsz, seq, n_mem):
    nblk = seq // WINDOW
    n = bsz * seq
    cur = lambda b, j: (b * nblk + j, 0)
    prev = lambda b, j: (b * nblk + jnp.maximum(j - 1, 0), 0)
    nxt = lambda b, j: (b * nblk + jnp.minimum(j + 1, nblk - 1), 0)
    memb = lambda b, j: (b, 0)
    return pl.pallas_call(
        functools.partial(_attn_kernel, seq=seq),
        grid=(bsz, nblk),
        in_specs=[
            pl.BlockSpec(memory_space=pltpu.SMEM),
            pl.BlockSpec((WINDOW, QW), cur),
            pl.BlockSpec((WINDOW, KVW), prev),
            pl.BlockSpec((WINDOW, KVW), cur),
            pl.BlockSpec((WINDOW, KVW), nxt),
            pl.BlockSpec((WINDOW, KVW), prev),
            pl.BlockSpec((WINDOW, KVW), cur),
            pl.BlockSpec((WINDOW, KVW), nxt),
            pl.BlockSpec((WINDOW, MQW), cur),
            pl.BlockSpec((n_mem, MQW), memb),
            pl.BlockSpec((n_mem, MQW), memb),
        ],
        out_specs=[pl.BlockSpec((WINDOW, QW), cur), pl.BlockSpec((WINDOW, MQW), cur)],
        out_shape=[jax.ShapeDtypeStruct((n, QW), bf16), jax.ShapeDtypeStruct((n, MQW), bf16)],
        compiler_params=_cparams(("arbitrary", "arbitrary")),
        name="attn",
    )(sink, q, k, k, k, v, v, v, mq, mk, mv)


def _merge_kernel(x_ref, g1_ref, a_ref, ao_ref, mo_ref, wpp_ref, wap_ref, wmp_ref,
                  wg0_ref, wg1_ref, wg2_ref, bg0_ref, bg1_ref, bg2_ref, wout_ref,
                  g2_ref, wrh_ref, wrl_ref, br_ref,
                  h_ref, x32_ref, lg_ref, xn_ref):
    j = pl.program_id(1)

    @pl.when(j == 0)
    def _():
        x = x_ref[...]
        xn_ref[...] = _rms(x, g1_ref[...]).astype(bf16)
        h_ref[...] = x

    xn = xn_ref[...]
    gate0 = jax.nn.sigmoid(_dot(xn, wg0_ref[...]) + bg0_ref[...])
    gate1 = jax.nn.sigmoid(_dot(xn, wg1_ref[...]) + bg1_ref[...])
    gate2 = jax.nn.sigmoid(_dot(xn, wg2_ref[...]) + bg2_ref[...])
    merged = (gate0 * _dot(a_ref[...], wpp_ref[...])
              + gate1 * _dot(ao_ref[...], wap_ref[...])
              + gate2 * _dot(mo_ref[...], wmp_ref[...]))
    h_ref[...] += _dot(merged.astype(bf16), wout_ref[...])

    @pl.when(j == pl.num_programs(1) - 1)
    def _():
        xn2 = _rms(h_ref[...], g2_ref[...])
        hi = xn2.astype(bf16)
        lo = (xn2 - hi.astype(f32)).astype(bf16)
        lg_ref[...] = (_dot_nt(wrh_ref[...], hi) + _dot_nt(wrl_ref[...], hi)
                       + _dot_nt(wrh_ref[...], lo) + br_ref[...])
        _pack_store(x32_ref, hi, TM_MG)


def _merge(x2, g1, a, ao, mo, wpp, wap, wmp, w_in_bf, b_gate, wout, g2, wr_hi, wr_lo, br_col):
    n = x2.shape[0]
    nt = D // TN_MG
    goff = A_WIDTH // TN_MG
    row = lambda i, j: (i, 0)
    col = lambda i, j: (0, j)
    const = lambda i, j: (0, 0)
    return pl.pallas_call(
        _merge_kernel,
        grid=(n // TM_MG, nt),
        in_specs=[
            pl.BlockSpec((TM_MG, D), row),
            pl.BlockSpec((1, D), const),
            pl.BlockSpec((TM_MG, POOL_W), row),
            pl.BlockSpec((TM_MG, QW), row),
            pl.BlockSpec((TM_MG, MQW), row),
            pl.BlockSpec((POOL_W, TN_MG), col),
            pl.BlockSpec((QW, TN_MG), col),
            pl.BlockSpec((MQW, TN_MG), col),
            pl.BlockSpec((D, TN_MG), lambda i, j: (0, goff + j)),
            pl.BlockSpec((D, TN_MG), lambda i, j: (0, goff + nt + j)),
            pl.BlockSpec((D, TN_MG), lambda i, j: (0, goff + 2 * nt + j)),
            pl.BlockSpec((1, TN_MG), lambda i, j: (0, j)),
            pl.BlockSpec((1, TN_MG), lambda i, j: (0, nt + j)),
            pl.BlockSpec((1, TN_MG), lambda i, j: (0, 2 * nt + j)),
            pl.BlockSpec((TN_MG, D), lambda i, j: (j, 0)),
            pl.BlockSpec((1, D), const),
            pl.BlockSpec((N_EXP, D), const),
            pl.BlockSpec((N_EXP, D), const),
            pl.BlockSpec((N_EXP, 1), const),
        ],
        out_specs=[
            pl.BlockSpec((TM_MG, D), row),
            pl.BlockSpec((TM_MG * SLAB, LANES), row),
            pl.BlockSpec((N_EXP, TM_MG), lambda i, j: (0, i)),
        ],
        out_shape=[
            jax.ShapeDtypeStruct((n, D), f32),
            jax.ShapeDtypeStruct((n * SLAB, LANES), u32),
            jax.ShapeDtypeStruct((N_EXP, n), f32),
        ],
        scratch_shapes=[pltpu.VMEM((TM_MG, D), bf16)],
        compiler_params=_cparams(("arbitrary", "arbitrary")),
        name="merge",
    )(x2, g1, a, ao, mo, wpp, wap, wmp, w_in_bf, w_in_bf, w_in_bf, b_gate, b_gate, b_gate, wout,
      g2, wr_hi, wr_lo, br_col)


def _route_kernel(lg_ref, tri_ref, dest_ref, w_ref, cnt_ref, cnt_scr, carry_scr):
    p = pl.program_id(0)
    i = pl.program_id(1)

    @pl.when((p == 0) & (i == 0))
    def _():
        cnt_scr[...] = jnp.zeros_like(cnt_scr)

    @pl.when(i == 0)
    def _():
        carry_scr[...] = jnp.zeros_like(carry_scr)

    l = lg_ref[...]
    eidx = lax.broadcasted_iota(i32, l.shape, 0)
    sels, tops = [], []
    for _ in range(TOP_K):
        m = jnp.max(l, axis=0, keepdims=True)
        idx = jnp.min(jnp.where(l == m, eidx, N_EXP), axis=0, keepdims=True)
        sel = eidx == idx
        l = jnp.where(sel, -jnp.inf, l)
        sels.append(sel)
        tops.append(m)
    es = [jnp.exp(t - tops[0]) for t in tops]
    tot = es[0] + es[1] + es[2] + es[3]
    w_ref[...] = jnp.concatenate([e / tot for e in es], axis=0)

    onehot = jnp.zeros(l.shape, f32)
    for sel in sels:
        onehot = jnp.where(sel, 1.0, onehot)
    blk_cnt = jnp.sum(onehot, axis=1, keepdims=True)

    @pl.when(p == 0)
    def _():
        cnt_scr[...] += blk_cnt

    cnt = cnt_scr[...][:, 0:1].astype(i32)
    padded = ((cnt + (MOE_BLOCK - 1)) // MOE_BLOCK * MOE_BLOCK).astype(f32)
    rr = lax.broadcasted_iota(i32, (N_EXP, N_EXP), 0)
    cc = lax.broadcasted_iota(i32, (N_EXP, N_EXP), 1)
    start_row = jnp.sum(jnp.where(rr < cc, padded, 0.0), axis=0, keepdims=True)
    start_col = jnp.sum(jnp.where(rr == cc, start_row, 0.0), axis=1, keepdims=True)
    cum = _dot(onehot.astype(bf16), tri_ref[...])
    val = cum + start_col + carry_scr[...][:, 0:1]
    dest_ref[...] = jnp.concatenate(
        [jnp.sum(jnp.where(sel, val, 0.0), axis=0, keepdims=True) for sel in sels], axis=0).astype(i32)
    carry_scr[...] += blk_cnt
    cnt_ref[...] = cnt_scr[...]


def _route(logits_t, tri):
    n = logits_t.shape[1]
    blk = lambda p, i: (0, i)
    final = lambda p, i: (0, i * p)
    return pl.pallas_call(
        _route_kernel,
        grid=(2, n // TR),
        in_specs=[pl.BlockSpec((N_EXP, TR), blk), pl.BlockSpec((TR, TR), lambda p, i: (0, 0))],
        out_specs=[
            pl.BlockSpec((TOP_K, TR), final),
            pl.BlockSpec((TOP_K, TR), final),
            pl.BlockSpec((N_EXP, LANES), lambda p, i: (0, 0)),
        ],
        out_shape=[
            jax.ShapeDtypeStruct((TOP_K, n), i32),
            jax.ShapeDtypeStruct((TOP_K, n), f32),
            jax.ShapeDtypeStruct((N_EXP, LANES), f32),
        ],
        scratch_shapes=[pltpu.VMEM((N_EXP, LANES), f32), pltpu.VMEM((N_EXP, LANES), f32)],
        compiler_params=_cparams(("arbitrary", "arbitrary")),
        name="route",
    )(logits_t, tri)


def _dispatch_kernel(dest_ref, x_hbm, init_hbm, xs_hbm, sem, *, n_tok):
    del init_hbm
    i = pl.program_id(0)
    base = i * TD

    def body(t, carry):
        tok = base + t
        src = x_hbm.at[pl.ds(pl.multiple_of(tok * SLAB, SLAB), SLAB)]
        for k in range(TOP_K):
            d = dest_ref[k * n_tok + tok]
            pltpu.make_async_copy(src, xs_hbm.at[pl.ds(pl.multiple_of(d * SLAB, SLAB), SLAB)], sem).start()
        return carry

    lax.fori_loop(0, TD, body, 0)

    def wait_block():
        rows = TOP_K * TD * SLAB
        pltpu.make_async_copy(x_hbm.at[pl.ds(0, rows)], xs_hbm.at[pl.ds(0, rows)], sem).wait()

    @pl.when(i > 0)
    def _():
        wait_block()

    @pl.when(i == pl.num_programs(0) - 1)
    def _():
        wait_block()


def _dispatch(dest_flat, x32, xs_init, n_tok):
    return pl.pallas_call(
        functools.partial(_dispatch_kernel, n_tok=n_tok),
        grid_spec=pltpu.PrefetchScalarGridSpec(
            num_scalar_prefetch=1,
            grid=(n_tok // TD,),
            in_specs=[pl.BlockSpec(memory_space=pl.ANY), pl.BlockSpec(memory_space=pl.ANY)],
            out_specs=pl.BlockSpec(memory_space=pl.ANY),
            scratch_shapes=[pltpu.SemaphoreType.DMA(())],
        ),
        out_shape=jax.ShapeDtypeStruct(xs_init.shape, u32),
        input_output_aliases={2: 0},
        compiler_params=_cparams(("arbitrary",)),
        name="dispatch",
    )(dest_flat, x32, xs_init)


def _repack_kernel(s_ref, o_ref):
    for c in range(SLAB):
        lo, hi = _unpack_chunk(s_ref[pl.ds(c, MOE_BLOCK, stride=SLAB), :])
        o_ref[:, c * LANES:(c + 1) * LANES] = lo.astype(bf16)
        o_ref[:, HALF + c * LANES:HALF + (c + 1) * LANES] = hi.astype(bf16)


def _repack(xs32, cap):
    return pl.pallas_call(
        _repack_kernel,
        grid=(cap // MOE_BLOCK,),
        in_specs=[pl.BlockSpec((MOE_BLOCK * SLAB, LANES), lambda i: (i, 0))],
        out_specs=pl.BlockSpec((MOE_BLOCK, D), lambda i: (i, 0)),
        out_shape=jax.ShapeDtypeStruct((cap, D), bf16),
        compiler_params=_cparams(("arbitrary",)),
        name="repack",
    )(xs32)


CAST_ROWS = 256


def _cast_weights(w_ref, wbf_ref):
    def body(r, carry):
        r0 = pl.multiple_of(r * CAST_ROWS, CAST_ROWS)
        wbf_ref[pl.ds(r0, CAST_ROWS), :] = w_ref[pl.ds(r0, CAST_ROWS), :].astype(bf16)
        return carry

    lax.fori_loop(0, w_ref.shape[0] // CAST_ROWS, body, 0)


def _expert_changed(be_ref, i):
    return (i == 0) | (be_ref[i] != be_ref[jnp.maximum(i - 1, 0)])


def _moe_up_kernel(be_ref, nb_ref, x_ref, w_ref, b_ref, sel_ref, act_ref, wbf_ref):
    i = pl.program_id(1)

    @pl.when(_expert_changed(be_ref, i))
    def _():
        _cast_weights(w_ref, wbf_ref)

    @pl.when(i < nb_ref[0])
    def _():
        hb = _dot(x_ref[...], wbf_ref[...]) + b_ref[...]
        gate = jnp.minimum(hb, LIMIT)
        up = jnp.clip(hb, -LIMIT, LIMIT)
        up_at_even = pltpu.roll(up, TN_UP - 1, 1)
        act = (gate * jax.nn.sigmoid(ALPHA * gate) * (up_at_even + 1.0)).astype(bf16)
        for c in range(TN_UP // (2 * LANES)):
            act_ref[:, c * LANES:(c + 1) * LANES] = _dot(
                act[:, c * 2 * LANES:(c + 1) * 2 * LANES], sel_ref[...]).astype(bf16)

    @pl.when(i >= nb_ref[0])
    def _():
        act_ref[...] = jnp.zeros_like(act_ref)


def _moe_up(blk_expert, nb_used, xs, w_up, b_up3, sel, cap):
    n_blocks = cap // MOE_BLOCK
    return pl.pallas_call(
        _moe_up_kernel,
        grid_spec=pltpu.PrefetchScalarGridSpec(
            num_scalar_prefetch=2,
            grid=(2 * D_FF // TN_UP, n_blocks),
            in_specs=[
                pl.BlockSpec((MOE_BLOCK, D), lambda j, i, be, nb: (i, 0)),
                pl.BlockSpec((None, D, TN_UP), lambda j, i, be, nb: (be[i], 0, j)),
                pl.BlockSpec((None, 1, TN_UP), lambda j, i, be, nb: (be[i], 0, j)),
                pl.BlockSpec((2 * LANES, LANES), lambda j, i, be, nb: (0, 0)),
            ],
            out_specs=pl.BlockSpec((MOE_BLOCK, TN_UP // 2), lambda j, i, be, nb: (i, j)),
            scratch_shapes=[pltpu.VMEM((D, TN_UP), bf16)],
        ),
        out_shape=jax.ShapeDtypeStruct((cap, D_FF), bf16),
        compiler_params=_cparams(("arbitrary", "arbitrary")),
        name="moe_up",
    )(blk_expert, nb_used, xs, w_up, b_up3, sel)


def _moe_down_kernel(be_ref, nb_ref, a_ref, w_ref, b_ref, y_ref, wbf_ref):
    i = pl.program_id(0)

    @pl.when(_expert_changed(be_ref, i))
    def _():
        _cast_weights(w_ref, wbf_ref)

    @pl.when(i < nb_ref[0])
    def _():
        y = _dot(a_ref[...], wbf_ref[...]) + b_ref[...]
        _pack_store(y_ref, y.astype(bf16), MOE_BLOCK)

    @pl.when(i >= nb_ref[0])
    def _():
        y_ref[...] = jnp.zeros_like(y_ref)


def _moe_down(blk_expert, nb_used, act, w_down, b_down3, cap):
    n_blocks = cap // MOE_BLOCK
    return pl.pallas_call(
        _moe_down_kernel,
        grid_spec=pltpu.PrefetchScalarGridSpec(
            num_scalar_prefetch=2,
            grid=(n_blocks,),
            in_specs=[
                pl.BlockSpec((MOE_BLOCK, D_FF), lambda i, be, nb: (i, 0)),
                pl.BlockSpec((None, D_FF, D), lambda i, be, nb: (be[i], 0, 0)),
                pl.BlockSpec((None, 1, D), lambda i, be, nb: (be[i], 0, 0)),
            ],
            out_specs=pl.BlockSpec((MOE_BLOCK * SLAB, LANES), lambda i, be, nb: (i, 0)),
            scratch_shapes=[pltpu.VMEM((D_FF, D), bf16)],
        ),
        out_shape=jax.ShapeDtypeStruct((cap * SLAB, LANES), u32),
        compiler_params=_cparams(("arbitrary",)),
        name="moe_down",
    )(blk_expert, nb_used, act, w_down, b_down3)


def _combine_kernel(dest_ref, y_hbm, h_ref, w_ref, o_ref, buf0, buf1, sem0, sem1, *, n_tok):
    i = pl.program_id(0)

    def issue(tb, buf, sem):
        base = tb * TC

        def body(t, carry):
            for k in range(TOP_K):
                d = dest_ref[k * n_tok + base + t]
                pltpu.make_async_copy(
                    y_hbm.at[pl.ds(pl.multiple_of(d * SLAB, SLAB), SLAB)],
                    buf.at[pl.ds(pl.multiple_of((k * TC + t) * SLAB, SLAB), SLAB)], sem).start()
            return carry

        lax.fori_loop(0, TC, body, 0)

    def wait(buf, sem):
        pltpu.make_async_copy(y_hbm.at[pl.ds(0, TOP_K * TC * SLAB)], buf, sem).wait()

    def compute(half, buf):
        rows = slice(half * TC, (half + 1) * TC)
        wv = w_ref[rows, :]
        for c in range(SLAB):
            cl = slice(c * LANES, (c + 1) * LANES)
            ch = slice(HALF + c * LANES, HALF + (c + 1) * LANES)
            acc_lo = h_ref[rows, cl]
            acc_hi = h_ref[rows, ch]
            for k in range(TOP_K):
                lo, hi = _unpack_chunk(buf[pl.ds(k * TC * SLAB + c, TC, stride=SLAB), :])
                acc_lo = acc_lo + wv[:, k:k + 1] * lo
                acc_hi = acc_hi + wv[:, k:k + 1] * hi
            o_ref[rows, cl] = acc_lo
            o_ref[rows, ch] = acc_hi

    @pl.when(i == 0)
    def _():
        issue(0, buf0, sem0)

    issue(2 * i + 1, buf1, sem1)
    wait(buf0, sem0)
    compute(0, buf0)

    @pl.when(i + 1 < pl.num_programs(0))
    def _():
        issue(2 * i + 2, buf0, sem0)

    wait(buf1, sem1)
    compute(1, buf1)


def _combine(dest_flat, y32, h, w_tok, n_tok):
    return pl.pallas_call(
        functools.partial(_combine_kernel, n_tok=n_tok),
        grid_spec=pltpu.PrefetchScalarGridSpec(
            num_scalar_prefetch=1,
            grid=(n_tok // (2 * TC),),
            in_specs=[
                pl.BlockSpec(memory_space=pl.ANY),
                pl.BlockSpec((2 * TC, D), lambda i, d: (i, 0)),
                pl.BlockSpec((2 * TC, TOP_K), lambda i, d: (i, 0)),
            ],
            out_specs=pl.BlockSpec((2 * TC, D), lambda i, d: (i, 0)),
            scratch_shapes=[
                pltpu.VMEM((TOP_K * TC * SLAB, LANES), u32),
                pltpu.VMEM((TOP_K * TC * SLAB, LANES), u32),
                pltpu.SemaphoreType.DMA(()),
                pltpu.SemaphoreType.DMA(()),
            ],
        ),
        out_shape=jax.ShapeDtypeStruct((n_tok, D), f32),
        compiler_params=_cparams(("arbitrary",)),
        name="combine",
    )(dest_flat, y32, h, w_tok)


def kernel(x, mem, positions, norm1_gain, w_in, b_gate, w_pool_group, pool_scale, w_pool_proj, attn_q_norm, attn_k_norm, attn_sink, w_attn_proj, mem_norm_gain, w_mem_kv, mem_q_norm, mem_k_norm, w_mem_proj, w_out, norm2_gain, w_router, b_router, w_up, b_up, w_down, b_down):
    bsz, seq, d = x.shape
    n_mem = mem.shape[1]
    n_tok = bsz * seq
    n_assign = n_tok * TOP_K
    n_blocks = -(-n_assign // MOE_BLOCK) + N_EXP
    cap = n_blocks * MOE_BLOCK
    depth = norm1_gain.shape[0]

    half = HD // 2
    inv_freq = jnp.power(jnp.float32(THETA), -jnp.arange(half, dtype=f32) * (2.0 / HD))
    invf = jnp.concatenate([inv_freq, inv_freq])[None, :]
    pos_col = positions.reshape(n_tok, 1)
    tri = jnp.asarray(np.triu(np.ones((TR, TR), np.float32), 1), bf16)
    sel_np = np.zeros((2 * LANES, LANES), np.float32)
    sel_np[2 * np.arange(LANES), np.arange(LANES)] = 1.0
    sel = jnp.asarray(sel_np, bf16)

    h2 = x.reshape(n_tok, d)
    mem2 = mem.reshape(bsz * n_mem, d)
    for l in range(depth):
        row = lambda v: v[l][None, :]
        w_in_bf = w_in[l].astype(bf16)
        pool_u, q, k, v, mq = _inproj(h2, row(norm1_gain), w_in_bf, pos_col, invf,
                                      row(attn_q_norm), row(attn_k_norm), row(mem_q_norm))
        a = _pool(pool_u.reshape(bsz, seq, POOL_W), w_pool_group[l].astype(bf16), row(pool_scale))
        mk, mv = _memkv(mem2, row(mem_norm_gain), w_mem_kv[l].astype(bf16), row(mem_k_norm), n_mem)
        ao, mo = _attn(attn_sink[l], q, k, v, mq, mk, mv, bsz, seq, n_mem)
        wr_t = w_router[l].T
        wr_hi = wr_t.astype(bf16)
        wr_lo = (wr_t - wr_hi.astype(f32)).astype(bf16)
        hmid, x32, logits_t = _merge(
            h2, row(norm1_gain), a.reshape(n_tok, POOL_W), ao, mo,
            w_pool_proj[l].astype(bf16), w_attn_proj[l].astype(bf16), w_mem_proj[l].astype(bf16),
            w_in_bf, row(b_gate), w_out[l].astype(bf16), row(norm2_gain), wr_hi, wr_lo,
            b_router[l][:, None])

        dest, w_top, cnt = _route(logits_t, tri)
        counts = cnt[:, 0].astype(i32)
        padded = (counts + MOE_BLOCK - 1) // MOE_BLOCK * MOE_BLOCK
        pad_ends = jnp.cumsum(padded)
        blk_start = jnp.arange(n_blocks, dtype=i32) * MOE_BLOCK
        blk_expert = jnp.minimum(jnp.searchsorted(pad_ends, blk_start, side='right'),
                                 N_EXP - 1).astype(i32)
        nb_used = (pad_ends[-1:] // MOE_BLOCK).astype(i32)
        dest_flat = dest.reshape(-1)

        xs32 = _dispatch(dest_flat, x32, jnp.zeros((cap * SLAB, LANES), u32), n_tok)
        xs = _repack(xs32, cap)
        act = _moe_up(blk_expert, nb_used, xs, w_up[l], b_up[l][:, None, :], sel, cap)
        y32 = _moe_down(blk_expert, nb_used, act, w_down[l], b_down[l][:, None, :], cap)
        h2 = _combine(dest_flat, y32, hmid, w_top.T, n_tok)
    return h2.reshape(bsz, seq, d)
```

```python
import functools

import numpy as np
import jax
import jax.numpy as jnp
from jax import lax
from jax.experimental import pallas as pl
from jax.experimental.pallas import tpu as pltpu

f32 = jnp.float32
bf16 = jnp.bfloat16
i32 = jnp.int32
u32 = jnp.uint32

D = 2048
HD = 128
POOL_WINDOWS = (2, 4, 8, 16)
POOL_W = 512
QW = 1024
KVW = 256
MQW = 512
A_WIDTH = POOL_W + QW + 2 * KVW + MQW
N_EXP = 32
TOP_K = 4
D_FF = 2048
LIMIT = 7.0
ALPHA = 1.702
MOE_BLOCK = 256
EPS = 1e-6
NEG = -1e30
THETA = 10000.0
WINDOW = 128

LANES = 128
SLAB = 8
HALF = D // 2
VMEM_LIMIT = 56 * 1024 * 1024

TM_IN = 512
TM_MG = 512
TN_MG = 256
TR = 512
TD = 512
TC = 128
TN_UP = 1024
POOL_CHUNK = 512
HALO = 8


def _cparams(sem):
    return pltpu.CompilerParams(dimension_semantics=sem, vmem_limit_bytes=VMEM_LIMIT)


def _dot(a, b):
    return jnp.dot(a, b, preferred_element_type=f32)


def _dot_nt(a, b):
    return lax.dot_general(a, b, (((1,), (1,)), ((), ())), preferred_element_type=f32)


def _rms(x, gain):
    return x * lax.rsqrt(jnp.mean(x * x, axis=-1, keepdims=True) + EPS) * gain


def _bits(x_f32):
    return lax.bitcast_convert_type(x_f32, u32)


def _unbits(x_u32):
    return lax.bitcast_convert_type(x_u32, f32)


def _pack_store(dst_ref, vals_bf16, rows):
    bits = _bits(vals_bf16.astype(f32))
    for c in range(SLAB):
        lo = bits[:, c * LANES:(c + 1) * LANES] >> 16
        hi = bits[:, HALF + c * LANES:HALF + (c + 1) * LANES] & jnp.uint32(0xFFFF0000)
        dst_ref[pl.ds(c, rows, stride=SLAB), :] = hi | lo


def _unpack_chunk(words):
    lo = _unbits(words << 16)
    hi = _unbits(words & jnp.uint32(0xFFFF0000))
    return lo, hi


def _inproj_kernel(x_ref, g1_ref, w_ref, pos_ref, invf_ref, qg_ref, kg_ref, mg_ref,
                   pool_ref, q_ref, k_ref, v_ref, mq_ref):
    x = x_ref[...]
    xn = _rms(x, g1_ref[...]).astype(bf16)
    ang = pos_ref[...].astype(f32) * invf_ref[...]
    cos = jnp.cos(ang)
    sin = jnp.sin(ang)
    lane = lax.broadcasted_iota(i32, ang.shape, 1)
    sin_s = jnp.where(lane < HD // 2, -sin, sin)

    def rope(u):
        return u * cos + pltpu.roll(u, HD // 2, 1) * sin_s

    pool_ref[...] = _dot(xn, w_ref[:, 0:POOL_W])
    o0 = POOL_W
    uq = _dot(xn, w_ref[:, o0:o0 + QW])
    for h in range(QW // HD):
        q_ref[:, h * HD:(h + 1) * HD] = rope(_rms(uq[:, h * HD:(h + 1) * HD], qg_ref[...])).astype(bf16)
    o0 += QW
    uk = _dot(xn, w_ref[:, o0:o0 + KVW])
    for h in range(KVW // HD):
        k_ref[:, h * HD:(h + 1) * HD] = rope(_rms(uk[:, h * HD:(h + 1) * HD], kg_ref[...])).astype(bf16)
    o0 += KVW
    v_ref[...] = _dot(xn, w_ref[:, o0:o0 + KVW]).astype(bf16)
    o0 += KVW
    um = _dot(xn, w_ref[:, o0:o0 + MQW])
    for h in range(MQW // HD):
        mq_ref[:, h * HD:(h + 1) * HD] = _rms(um[:, h * HD:(h + 1) * HD], mg_ref[...]).astype(bf16)


def _inproj(x2, g1, w_in_bf, pos_col, invf, qg, kg, mg):
    n = x2.shape[0]
    row = lambda i: (i, 0)
    const = lambda i: (0, 0)
    return pl.pallas_call(
        _inproj_kernel,
        grid=(n // TM_IN,),
        in_specs=[
            pl.BlockSpec((TM_IN, D), row),
            pl.BlockSpec((1, D), const),
            pl.BlockSpec((D, A_WIDTH), const),
            pl.BlockSpec((TM_IN, 1), row),
            pl.BlockSpec((1, HD), const),
            pl.BlockSpec((1, HD), const),
            pl.BlockSpec((1, HD), const),
            pl.BlockSpec((1, HD), const),
        ],
        out_specs=[
            pl.BlockSpec((TM_IN, POOL_W), row),
            pl.BlockSpec((TM_IN, QW), row),
            pl.BlockSpec((TM_IN, KVW), row),
            pl.BlockSpec((TM_IN, KVW), row),
            pl.BlockSpec((TM_IN, MQW), row),
        ],
        out_shape=[
            jax.ShapeDtypeStruct((n, POOL_W), f32),
            jax.ShapeDtypeStruct((n, QW), bf16),
            jax.ShapeDtypeStruct((n, KVW), bf16),
            jax.ShapeDtypeStruct((n, KVW), bf16),
            jax.ShapeDtypeStruct((n, MQW), bf16),
        ],
        compiler_params=_cparams(("arbitrary",)),
        name="inproj",
    )(x2, g1, w_in_bf, pos_col, invf, qg, kg, mg)


def _pool_kernel(u_ref, wg_ref, sc_ref, a_ref, pad_ref, *, seq):
    zeros = jnp.zeros((HALO, POOL_W), f32)
    pad_ref[0:HALO, :] = zeros
    pad_ref[seq + HALO:seq + 2 * HALO, :] = zeros
    pad_ref[HALO:seq + HALO, :] = u_ref[0]

    def chunk(ci, carry):
        c0 = pl.multiple_of(ci * POOL_CHUNK, POOL_CHUNK)
        pos = c0 + lax.broadcasted_iota(i32, (POOL_CHUNK, 1), 0)
        for g, w in enumerate(POOL_WINDOWS):
            ext = pad_ref[pl.ds(c0, POOL_CHUNK + 2 * HALO), g * HD:(g + 1) * HD]
            tok = ext[HALO:HALO + POOL_CHUNK]
            win = None
            for o in range(-(w // 2), w // 2):
                term = ext[HALO + o:HALO + o + POOL_CHUNK]
                win = term if win is None else win + term
            lo = jnp.clip(pos - w // 2, 0, seq)
            hi = jnp.clip(pos - w // 2 + w, 0, seq)
            cnt = (hi - lo).astype(f32)
            pooled = (win / cnt - tok).astype(bf16)
            a = _dot(pooled, wg_ref[g]) * sc_ref[:, g * HD:(g + 1) * HD]
            a_ref[0, pl.ds(c0, POOL_CHUNK), g * HD:(g + 1) * HD] = a.astype(bf16)
        return carry

    lax.fori_loop(0, seq // POOL_CHUNK, chunk, 0)


def _pool(u3, wg_bf, scale):
    b, seq, _ = u3.shape
    return pl.pallas_call(
        functools.partial(_pool_kernel, seq=seq),
        grid=(b,),
        in_specs=[
            pl.BlockSpec((1, seq, POOL_W), lambda i: (i, 0, 0)),
            pl.BlockSpec((len(POOL_WINDOWS), HD, HD), lambda i: (0, 0, 0)),
            pl.BlockSpec((1, POOL_W), lambda i: (0, 0)),
        ],
        out_specs=pl.BlockSpec((1, seq, POOL_W), lambda i: (i, 0, 0)),
        out_shape=jax.ShapeDtypeStruct((b, seq, POOL_W), bf16),
        scratch_shapes=[pltpu.VMEM((seq + 2 * HALO, POOL_W), f32)],
        compiler_params=_cparams(("arbitrary",)),
        name="pool",
    )(u3, wg_bf, scale)


def _memkv_kernel(m_ref, g_ref, w_ref, kg_ref, mk_ref, mv_ref):
    mn = _rms(m_ref[...], g_ref[...]).astype(bf16)
    kv = _dot(mn, w_ref[...])
    for h in range(MQW // HD):
        mk_ref[:, h * HD:(h + 1) * HD] = _rms(kv[:, h * HD:(h + 1) * HD], kg_ref[...]).astype(bf16)
    mv_ref[...] = kv[:, MQW:].astype(bf16)


def _memkv(mem2, gain, w_bf, kg, n_mem):
    rows = mem2.shape[0]
    return pl.pallas_call(
        _memkv_kernel,
        grid=(rows // n_mem,),
        in_specs=[
            pl.BlockSpec((n_mem, D), lambda i: (i, 0)),
            pl.BlockSpec((1, D), lambda i: (0, 0)),
            pl.BlockSpec((D, 2 * MQW), lambda i: (0, 0)),
            pl.BlockSpec((1, HD), lambda i: (0, 0)),
        ],
        out_specs=[pl.BlockSpec((n_mem, MQW), lambda i: (i, 0))] * 2,
        out_shape=[jax.ShapeDtypeStruct((rows, MQW), bf16)] * 2,
        compiler_params=_cparams(("arbitrary",)),
        name="memkv",
    )(mem2, gain, w_bf, kg)


def _attn_kernel(sink_ref, q_ref, kp_ref, kc_ref, kn_ref, vp_ref, vc_ref, vn_ref, mq_ref, mk_ref, mv_ref,
                 ao_ref, mo_ref, *, seq):
    blk = WINDOW
    nb = pl.program_id(1)
    scale = 1.0 / np.sqrt(HD)
    group = (QW // HD) // (KVW // HD)
    rows = group * blk

    r = lax.broadcasted_iota(i32, (rows, 3 * blk), 0)
    c = lax.broadcasted_iota(i32, (rows, 3 * blk), 1)
    qi = (r % blk) + blk
    key_abs = (nb - 1) * blk + c
    mask = (jnp.abs(c - qi) <= WINDOW) & (key_abs >= 0) & (key_abs < seq)
    rcol = lax.broadcasted_iota(i32, (rows, 1), 0) // blk

    for kh in range(KVW // HD):
        ks = slice(kh * HD, (kh + 1) * HD)
        kb = jnp.concatenate([kp_ref[:, ks], kc_ref[:, ks], kn_ref[:, ks]], axis=0)
        vb = jnp.concatenate([vp_ref[:, ks], vc_ref[:, ks], vn_ref[:, ks]], axis=0)
        qh = jnp.concatenate(
            [q_ref[:, (kh * group + g) * HD:(kh * group + g + 1) * HD] for g in range(group)], axis=0)
        s = _dot_nt(qh, kb) * scale
        s = jnp.where(mask, s, NEG)
        sk = jnp.zeros((rows, 1), f32)
        for g in range(group):
            sk = jnp.where(rcol == g, sink_ref[kh * group + g], sk)
        m = jnp.maximum(jnp.max(s, axis=-1, keepdims=True), sk)
        p = jnp.exp(s - m)
        denom = jnp.sum(p, axis=-1, keepdims=True) + jnp.exp(sk - m)
        probs = (p / denom).astype(bf16)
        o = _dot(probs, vb)
        for g in range(group):
            h = kh * group + g
            ao_ref[:, h * HD:(h + 1) * HD] = o[g * blk:(g + 1) * blk].astype(bf16)

    for h in range(MQW // HD):
        hs = slice(h * HD, (h + 1) * HD)
        s = _dot_nt(mq_ref[:, hs], mk_ref[:, hs]) * scale
        m = jnp.max(s, axis=-1, keepdims=True)
        p = jnp.exp(s - m)
        probs = (p / jnp.sum(p, axis=-1, keepdims=True)).astype(bf16)
        mo_ref[:, hs] = _dot(probs, mv_ref[:, hs]).astype(bf16)


def _attn(sink, q, k, v, mq, mk, mv, bsz, seq, n_mem):
    nblk = seq // WINDOW
    n = bsz * seq
    cur = lambda b, j: (b * nblk + j, 0)
    prev = lambda b, j: (b * nblk + jnp.maximum(j - 1, 0), 0)
    nxt = lambda b, j: (b * nblk + jnp.minimum(j + 1, nblk - 1), 0)
    memb = lambda b, j: (b, 0)
    return pl.pallas_call(
        functools.partial(_attn_kernel, seq=seq),
        grid=(bsz, nblk),
        in_specs=[
            pl.BlockSpec(memory_space=pltpu.SMEM),
            pl.BlockSpec((WINDOW, QW), cur),
            pl.BlockSpec((WINDOW, KVW), prev),
            pl.BlockSpec((WINDOW, KVW), cur),
            pl.BlockSpec((WINDOW, KVW), nxt),
            pl.BlockSpec((WINDOW, KVW), prev),
            pl.BlockSpec((WINDOW, KVW), cur),
            pl.BlockSpec((WINDOW, KVW), nxt),
            pl.BlockSpec((WINDOW, MQW), cur),
            pl.BlockSpec((n_mem, MQW), memb),
            pl.BlockSpec((n_mem, MQW), memb),
        ],
        out_specs=[pl.BlockSpec((WINDOW, QW), cur), pl.BlockSpec((WINDOW, MQW), cur)],
        out_shape=[jax.ShapeDtypeStruct((n, QW), bf16), jax.ShapeDtypeStruct((n, MQW), bf16)],
        compiler_params=_cparams(("arbitrary", "arbitrary")),
        name="attn",
    )(sink, q, k, k, k, v, v, v, mq, mk, mv)


def _merge_kernel(x_ref, g1_ref, a_ref, ao_ref, mo_ref, wpp_ref, wap_ref, wmp_ref,
                  wg0_ref, wg1_ref, wg2_ref, bg0_ref, bg1_ref, bg2_ref, wout_ref,
                  g2_ref, wrh_ref, wrl_ref, br_ref,
                  h_ref, x32_ref, lg_ref, xn_ref):
    j = pl.program_id(1)

    @pl.when(j == 0)
    def _():
        x = x_ref[...]
        xn_ref[...] = _rms(x, g1_ref[...]).astype(bf16)
        h_ref[...] = x

    xn = xn_ref[...]
    gate0 = jax.nn.sigmoid(_dot(xn, wg0_ref[...]) + bg0_ref[...])
    gate1 = jax.nn.sigmoid(_dot(xn, wg1_ref[...]) + bg1_ref[...])
    gate2 = jax.nn.sigmoid(_dot(xn, wg2_ref[...]) + bg2_ref[...])
    merged = (gate0 * _dot(a_ref[...], wpp_ref[...])
              + gate1 * _dot(ao_ref[...], wap_ref[...])
              + gate2 * _dot(mo_ref[...], wmp_ref[...]))
    h_ref[...] += _dot(merged.astype(bf16), wout_ref[...])

    @pl.when(j == pl.num_programs(1) - 1)
    def _():
        xn2 = _rms(h_ref[...], g2_ref[...])
        hi = xn2.astype(bf16)
        lo = (xn2 - hi.astype(f32)).astype(bf16)
        lg_ref[...] = (_dot_nt(wrh_ref[...], hi) + _dot_nt(wrl_ref[...], hi)
                       + _dot_nt(wrh_ref[...], lo) + br_ref[...])
        _pack_store(x32_ref, hi, TM_MG)


def _merge(x2, g1, a, ao, mo, wpp, wap, wmp, w_in_bf, b_gate, wout, g2, wr_hi, wr_lo, br_col):
    n = x2.shape[0]
    nt = D // TN_MG
    goff = A_WIDTH // TN_MG
    row = lambda i, j: (i, 0)
    col = lambda i, j: (0, j)
    const = lambda i, j: (0, 0)
    return pl.pallas_call(
        _merge_kernel,
        grid=(n // TM_MG, nt),
        in_specs=[
            pl.BlockSpec((TM_MG, D), row),
            pl.BlockSpec((1, D), const),
            pl.BlockSpec((TM_MG, POOL_W), row),
            pl.BlockSpec((TM_MG, QW), row),
            pl.BlockSpec((TM_MG, MQW), row),
            pl.BlockSpec((POOL_W, TN_MG), col),
            pl.BlockSpec((QW, TN_MG), col),
            pl.BlockSpec((MQW, TN_MG), col),
            pl.BlockSpec((D, TN_MG), lambda i, j: (0, goff + j)),
            pl.BlockSpec((D, TN_MG), lambda i, j: (0, goff + nt + j)),
            pl.BlockSpec((D, TN_MG), lambda i, j: (0, goff + 2 * nt + j)),
            pl.BlockSpec((1, TN_MG), lambda i, j: (0, j)),
            pl.BlockSpec((1, TN_MG), lambda i, j: (0, nt + j)),
            pl.BlockSpec((1, TN_MG), lambda i, j: (0, 2 * nt + j)),
            pl.BlockSpec((TN_MG, D), lambda i, j: (j, 0)),
            pl.BlockSpec((1, D), const),
            pl.BlockSpec((N_EXP, D), const),
            pl.BlockSpec((N_EXP, D), const),
            pl.BlockSpec((N_EXP, 1), const),
        ],
        out_specs=[
            pl.BlockSpec((TM_MG, D), row),
            pl.BlockSpec((TM_MG * SLAB, LANES), row),
            pl.BlockSpec((N_EXP, TM_MG), lambda i, j: (0, i)),
        ],
        out_shape=[
            jax.ShapeDtypeStruct((n, D), f32),
            jax.ShapeDtypeStruct((n * SLAB, LANES), u32),
            jax.ShapeDtypeStruct((N_EXP, n), f32),
        ],
        scratch_shapes=[pltpu.VMEM((TM_MG, D), bf16)],
        compiler_params=_cparams(("arbitrary", "arbitrary")),
        name="merge",
    )(x2, g1, a, ao, mo, wpp, wap, wmp, w_in_bf, w_in_bf, w_in_bf, b_gate, b_gate, b_gate, wout,
      g2, wr_hi, wr_lo, br_col)


def _route_kernel(lg_ref, tri_ref, dest_ref, w_ref, cnt_ref, cnt_scr, carry_scr):
    p = pl.program_id(0)
    i = pl.program_id(1)

    @pl.when((p == 0) & (i == 0))
    def _():
        cnt_scr[...] = jnp.zeros_like(cnt_scr)

    @pl.when(i == 0)
    def _():
        carry_scr[...] = jnp.zeros_like(carry_scr)

    l = lg_ref[...]
    eidx = lax.broadcasted_iota(i32, l.shape, 0)
    sels, tops = [], []
    for _ in range(TOP_K):
        m = jnp.max(l, axis=0, keepdims=True)
        idx = jnp.min(jnp.where(l == m, eidx, N_EXP), axis=0, keepdims=True)
        sel = eidx == idx
        l = jnp.where(sel, -jnp.inf, l)
        sels.append(sel)
        tops.append(m)
    es = [jnp.exp(t - tops[0]) for t in tops]
    tot = es[0] + es[1] + es[2] + es[3]
    w_ref[...] = jnp.concatenate([e / tot for e in es], axis=0)

    onehot = jnp.zeros(l.shape, f32)
    for sel in sels:
        onehot = jnp.where(sel, 1.0, onehot)
    blk_cnt = jnp.sum(onehot, axis=1, keepdims=True)

    @pl.when(p == 0)
    def _():
        cnt_scr[...] += blk_cnt

    cnt = cnt_scr[...][:, 0:1].astype(i32)
    padded = ((cnt + (MOE_BLOCK - 1)) // MOE_BLOCK * MOE_BLOCK).astype(f32)
    rr = lax.broadcasted_iota(i32, (N_EXP, N_EXP), 0)
    cc = lax.broadcasted_iota(i32, (N_EXP, N_EXP), 1)
    start_row = jnp.sum(jnp.where(rr < cc, padded, 0.0), axis=0, keepdims=True)
    start_col = jnp.sum(jnp.where(rr == cc, start_row, 0.0), axis=1, keepdims=True)
    cum = _dot(onehot.astype(bf16), tri_ref[...])
    val = cum + start_col + carry_scr[...][:, 0:1]
    dest_ref[...] = jnp.concatenate(
        [jnp.sum(jnp.where(sel, val, 0.0), axis=0, keepdims=True) for sel in sels], axis=0).astype(i32)
    carry_scr[...] += blk_cnt
    cnt_ref[...] = cnt_scr[...]


def _route(logits_t, tri):
    n = logits_t.shape[1]
    blk = lambda p, i: (0, i)
    final = lambda p, i: (0, i * p)
    return pl.pallas_call(
        _route_kernel,
        grid=(2, n // TR),
        in_specs=[pl.BlockSpec((N_EXP, TR), blk), pl.BlockSpec((TR, TR), lambda p, i: (0, 0))],
        out_specs=[
            pl.BlockSpec((TOP_K, TR), final),
            pl.BlockSpec((TOP_K, TR), final),
            pl.BlockSpec((N_EXP, LANES), lambda p, i: (0, 0)),
        ],
        out_shape=[
            jax.ShapeDtypeStruct((TOP_K, n), i32),
            jax.ShapeDtypeStruct((TOP_K, n), f32),
            jax.ShapeDtypeStruct((N_EXP, LANES), f32),
        ],
        scratch_shapes=[pltpu.VMEM((N_EXP, LANES), f32), pltpu.VMEM((N_EXP, LANES), f32)],
        compiler_params=_cparams(("arbitrary", "arbitrary")),
        name="route",
    )(logits_t, tri)


def _dispatch_kernel(dest_ref, x_ref, init_hbm, xs_hbm, sem, *, n_tok):
    del init_hbm
    base = pl.program_id(0) * TD

    def body(t, carry):
        src = x_ref.at[pl.ds(pl.multiple_of(t * SLAB, SLAB), SLAB)]
        for k in range(TOP_K):
            d = dest_ref[k * n_tok + base + t]
            pltpu.make_async_copy(src, xs_hbm.at[pl.ds(pl.multiple_of(d * SLAB, SLAB), SLAB)], sem).start()
        return carry

    lax.fori_loop(0, TD, body, 0)
    rows = TOP_K * TD * SLAB
    pltpu.make_async_copy(xs_hbm.at[pl.ds(0, rows)], xs_hbm.at[pl.ds(0, rows)], sem).wait()


def _dispatch(dest_flat, x32, xs_init, n_tok):
    return pl.pallas_call(
        functools.partial(_dispatch_kernel, n_tok=n_tok),
        grid_spec=pltpu.PrefetchScalarGridSpec(
            num_scalar_prefetch=1,
            grid=(n_tok // TD,),
            in_specs=[pl.BlockSpec((TD * SLAB, LANES), lambda i, d: (i, 0)), pl.BlockSpec(memory_space=pl.ANY)],
            out_specs=pl.BlockSpec(memory_space=pl.ANY),
            scratch_shapes=[pltpu.SemaphoreType.DMA(())],
        ),
        out_shape=jax.ShapeDtypeStruct(xs_init.shape, u32),
        input_output_aliases={2: 0},
        compiler_params=_cparams(("arbitrary",)),
        name="dispatch",
    )(dest_flat, x32, xs_init)


def _repack_kernel(s_ref, o_ref):
    for c in range(SLAB):
        lo, hi = _unpack_chunk(s_ref[pl.ds(c, MOE_BLOCK, stride=SLAB), :])
        o_ref[:, c * LANES:(c + 1) * LANES] = lo.astype(bf16)
        o_ref[:, HALF + c * LANES:HALF + (c + 1) * LANES] = hi.astype(bf16)


def _repack(xs32, cap):
    return pl.pallas_call(
        _repack_kernel,
        grid=(cap // MOE_BLOCK,),
        in_specs=[pl.BlockSpec((MOE_BLOCK * SLAB, LANES), lambda i: (i, 0))],
        out_specs=pl.BlockSpec((MOE_BLOCK, D), lambda i: (i, 0)),
        out_shape=jax.ShapeDtypeStruct((cap, D), bf16),
        compiler_params=_cparams(("arbitrary",)),
        name="repack",
    )(xs32)


CAST_ROWS = 256


def _cast_weights(w_ref, wbf_ref):
    def body(r, carry):
        r0 = pl.multiple_of(r * CAST_ROWS, CAST_ROWS)
        wbf_ref[pl.ds(r0, CAST_ROWS), :] = w_ref[pl.ds(r0, CAST_ROWS), :].astype(bf16)
        return carry

    lax.fori_loop(0, w_ref.shape[0] // CAST_ROWS, body, 0)


def _expert_changed(be_ref, i):
    return (i == 0) | (be_ref[i] != be_ref[jnp.maximum(i - 1, 0)])


def _moe_up_kernel(be_ref, nb_ref, x_ref, w_ref, b_ref, sel_ref, act_ref, wbf_ref):
    i = pl.program_id(1)

    @pl.when(_expert_changed(be_ref, i))
    def _():
        _cast_weights(w_ref, wbf_ref)

    @pl.when(i < nb_ref[0])
    def _():
        hb = _dot(x_ref[...], wbf_ref[...]) + b_ref[...]
        gate = jnp.minimum(hb, LIMIT)
        up = jnp.clip(hb, -LIMIT, LIMIT)
        up_at_even = pltpu.roll(up, TN_UP - 1, 1)
        act = (gate * jax.nn.sigmoid(ALPHA * gate) * (up_at_even + 1.0)).astype(bf16)
        for c in range(TN_UP // (2 * LANES)):
            act_ref[:, c * LANES:(c + 1) * LANES] = _dot(
                act[:, c * 2 * LANES:(c + 1) * 2 * LANES], sel_ref[...]).astype(bf16)

    @pl.when(i >= nb_ref[0])
    def _():
        act_ref[...] = jnp.zeros_like(act_ref)


def _moe_up(blk_expert, nb_used, xs, w_up, b_up3, sel, cap):
    n_blocks = cap // MOE_BLOCK
    return pl.pallas_call(
        _moe_up_kernel,
        grid_spec=pltpu.PrefetchScalarGridSpec(
            num_scalar_prefetch=2,
            grid=(2 * D_FF // TN_UP, n_blocks),
            in_specs=[
                pl.BlockSpec((MOE_BLOCK, D), lambda j, i, be, nb: (i, 0)),
                pl.BlockSpec((None, D, TN_UP), lambda j, i, be, nb: (be[i], 0, j)),
                pl.BlockSpec((None, 1, TN_UP), lambda j, i, be, nb: (be[i], 0, j)),
                pl.BlockSpec((2 * LANES, LANES), lambda j, i, be, nb: (0, 0)),
            ],
            out_specs=pl.BlockSpec((MOE_BLOCK, TN_UP // 2), lambda j, i, be, nb: (i, j)),
            scratch_shapes=[pltpu.VMEM((D, TN_UP), bf16)],
        ),
        out_shape=jax.ShapeDtypeStruct((cap, D_FF), bf16),
        compiler_params=_cparams(("arbitrary", "arbitrary")),
        name="moe_up",
    )(blk_expert, nb_used, xs, w_up, b_up3, sel)


def _moe_down_kernel(be_ref, nb_ref, a_ref, w_ref, b_ref, y_ref, wbf_ref):
    i = pl.program_id(0)

    @pl.when(_expert_changed(be_ref, i))
    def _():
        _cast_weights(w_ref, wbf_ref)

    @pl.when(i < nb_ref[0])
    def _():
        y = _dot(a_ref[...], wbf_ref[...]) + b_ref[...]
        _pack_store(y_ref, y.astype(bf16), MOE_BLOCK)

    @pl.when(i >= nb_ref[0])
    def _():
        y_ref[...] = jnp.zeros_like(y_ref)


def _moe_down(blk_expert, nb_used, act, w_down, b_down3, cap):
    n_blocks = cap // MOE_BLOCK
    return pl.pallas_call(
        _moe_down_kernel,
        grid_spec=pltpu.PrefetchScalarGridSpec(
            num_scalar_prefetch=2,
            grid=(n_blocks,),
            in_specs=[
                pl.BlockSpec((MOE_BLOCK, D_FF), lambda i, be, nb: (i, 0)),
                pl.BlockSpec((None, D_FF, D), lambda i, be, nb: (be[i], 0, 0)),
                pl.BlockSpec((None, 1, D), lambda i, be, nb: (be[i], 0, 0)),
            ],
            out_specs=pl.BlockSpec((MOE_BLOCK * SLAB, LANES), lambda i, be, nb: (i, 0)),
            scratch_shapes=[pltpu.VMEM((D_FF, D), bf16)],
        ),
        out_shape=jax.ShapeDtypeStruct((cap * SLAB, LANES), u32),
        compiler_params=_cparams(("arbitrary",)),
        name="moe_down",
    )(blk_expert, nb_used, act, w_down, b_down3)


def _combine_kernel(dest_ref, y_hbm, h_ref, w_ref, o_ref, buf0, buf1, sem0, sem1, *, n_tok):
    i = pl.program_id(0)

    def issue(tb, buf, sem):
        base = tb * TC

        def body(t, carry):
            for k in range(TOP_K):
                d = dest_ref[k * n_tok + base + t]
                pltpu.make_async_copy(
                    y_hbm.at[pl.ds(pl.multiple_of(d * SLAB, SLAB), SLAB)],
                    buf.at[pl.ds(pl.multiple_of((k * TC + t) * SLAB, SLAB), SLAB)], sem).start()
            return carry

        lax.fori_loop(0, TC, body, 0)

    def wait(buf, sem):
        pltpu.make_async_copy(y_hbm.at[pl.ds(0, TOP_K * TC * SLAB)], buf, sem).wait()

    def compute(half, buf):
        rows = slice(half * TC, (half + 1) * TC)
        wv = w_ref[rows, :]
        for c in range(SLAB):
            cl = slice(c * LANES, (c + 1) * LANES)
            ch = slice(HALF + c * LANES, HALF + (c + 1) * LANES)
            acc_lo = h_ref[rows, cl]
            acc_hi = h_ref[rows, ch]
            for k in range(TOP_K):
                lo, hi = _unpack_chunk(buf[pl.ds(k * TC * SLAB + c, TC, stride=SLAB), :])
                acc_lo = acc_lo + wv[:, k:k + 1] * lo
                acc_hi = acc_hi + wv[:, k:k + 1] * hi
            o_ref[rows, cl] = acc_lo
            o_ref[rows, ch] = acc_hi

    @pl.when(i == 0)
    def _():
        issue(0, buf0, sem0)

    issue(2 * i + 1, buf1, sem1)
    wait(buf0, sem0)
    compute(0, buf0)

    @pl.when(i + 1 < pl.num_programs(0))
    def _():
        issue(2 * i + 2, buf0, sem0)

    wait(buf1, sem1)
    compute(1, buf1)


def _combine(dest_flat, y32, h, w_tok, n_tok):
    return pl.pallas_call(
        functools.partial(_combine_kernel, n_tok=n_tok),
        grid_spec=pltpu.PrefetchScalarGridSpec(
            num_scalar_prefetch=1,
            grid=(n_tok // (2 * TC),),
            in_specs=[
                pl.BlockSpec(memory_space=pl.ANY),
                pl.BlockSpec((2 * TC, D), lambda i, d: (i, 0)),
                pl.BlockSpec((2 * TC, TOP_K), lambda i, d: (i, 0)),
            ],
            out_specs=pl.BlockSpec((2 * TC, D), lambda i, d: (i, 0)),
            scratch_shapes=[
                pltpu.VMEM((TOP_K * TC * SLAB, LANES), u32),
                pltpu.VMEM((TOP_K * TC * SLAB, LANES), u32),
                pltpu.SemaphoreType.DMA(()),
                pltpu.SemaphoreType.DMA(()),
            ],
        ),
        out_shape=jax.ShapeDtypeStruct((n_tok, D), f32),
        compiler_params=_cparams(("arbitrary",)),
        name="combine",
    )(dest_flat, y32, h, w_tok)


def kernel(x, mem, positions, norm1_gain, w_in, b_gate, w_pool_group, pool_scale, w_pool_proj, attn_q_norm, attn_k_norm, attn_sink, w_attn_proj, mem_norm_gain, w_mem_kv, mem_q_norm, mem_k_norm, w_mem_proj, w_out, norm2_gain, w_router, b_router, w_up, b_up, w_down, b_down):
    bsz, seq, d = x.shape
    n_mem = mem.shape[1]
    n_tok = bsz * seq
    n_assign = n_tok * TOP_K
    n_blocks = -(-n_assign // MOE_BLOCK) + N_EXP
    cap = n_blocks * MOE_BLOCK
    depth = norm1_gain.shape[0]

    half = HD // 2
    inv_freq = jnp.power(jnp.float32(THETA), -jnp.arange(half, dtype=f32) * (2.0 / HD))
    invf = jnp.concatenate([inv_freq, inv_freq])[None, :]
    pos_col = positions.reshape(n_tok, 1)
    tri = jnp.asarray(np.triu(np.ones((TR, TR), np.float32), 1), bf16)
    sel_np = np.zeros((2 * LANES, LANES), np.float32)
    sel_np[2 * np.arange(LANES), np.arange(LANES)] = 1.0
    sel = jnp.asarray(sel_np, bf16)

    h2 = x.reshape(n_tok, d)
    mem2 = mem.reshape(bsz * n_mem, d)
    for l in range(depth):
        row = lambda v: v[l][None, :]
        w_in_bf = w_in[l].astype(bf16)
        pool_u, q, k, v, mq = _inproj(h2, row(norm1_gain), w_in_bf, pos_col, invf,
                                      row(attn_q_norm), row(attn_k_norm), row(mem_q_norm))
        a = _pool(pool_u.reshape(bsz, seq, POOL_W), w_pool_group[l].astype(bf16), row(pool_scale))
        mk, mv = _memkv(mem2, row(mem_norm_gain), w_mem_kv[l].astype(bf16), row(mem_k_norm), n_mem)
        ao, mo = _attn(attn_sink[l], q, k, v, mq, mk, mv, bsz, seq, n_mem)
        wr_t = w_router[l].T
        wr_hi = wr_t.astype(bf16)
        wr_lo = (wr_t - wr_hi.astype(f32)).astype(bf16)
        hmid, x32, logits_t = _merge(
            h2, row(norm1_gain), a.reshape(n_tok, POOL_W), ao, mo,
            w_pool_proj[l].astype(bf16), w_attn_proj[l].astype(bf16), w_mem_proj[l].astype(bf16),
            w_in_bf, row(b_gate), w_out[l].astype(bf16), row(norm2_gain), wr_hi, wr_lo,
            b_router[l][:, None])

        dest, w_top, cnt = _route(logits_t, tri)
        counts = cnt[:, 0].astype(i32)
        padded = (counts + MOE_BLOCK - 1) // MOE_BLOCK * MOE_BLOCK
        pad_ends = jnp.cumsum(padded)
        blk_start = jnp.arange(n_blocks, dtype=i32) * MOE_BLOCK
        blk_expert = jnp.minimum(jnp.sum((pad_ends[None, :] <= blk_start[:, None]).astype(i32), axis=1),
                                 N_EXP - 1)
        nb_used = (pad_ends[-1:] // MOE_BLOCK).astype(i32)
        dest_flat = dest.reshape(-1)

        xs32 = _dispatch(dest_flat, x32, jnp.zeros((cap * SLAB, LANES), u32), n_tok)
        xs = _repack(xs32, cap)
        act = _moe_up(blk_expert, nb_used, xs, w_up[l], b_up[l][:, None, :], sel, cap)
        y32 = _moe_down(blk_expert, nb_used, act, w_down[l], b_down[l][:, None, :], cap)
        h2 = _combine(dest_flat, y32, hmid, w_top.T, n_tok)
    return h2.reshape(bsz, seq, d)
```

```python
import functools

import numpy as np
import jax
import jax.numpy as jnp
from jax import lax
from jax.experimental import pallas as pl
from jax.experimental.pallas import tpu as pltpu

f32 = jnp.float32
bf16 = jnp.bfloat16
i32 = jnp.int32
u32 = jnp.uint32

D = 2048
HD = 128
POOL_WINDOWS = (2, 4, 8, 16)
POOL_W = 512
QW = 1024
KVW = 256
MQW = 512
A_WIDTH = POOL_W + QW + 2 * KVW + MQW
N_EXP = 32
TOP_K = 4
D_FF = 2048
LIMIT = 7.0
ALPHA = 1.702
MOE_BLOCK = 512
EPS = 1e-6
NEG = -1e30
THETA = 10000.0
WINDOW = 128

LANES = 128
SLAB = 8
HALF = D // 2
VMEM_LIMIT = 56 * 1024 * 1024

TM_IN = 512
TM_MG = 512
TN_MG = 256
TR = 512
TD = 512
TC = 128
TN_UP = 2048
UP_CHUNK = 512
POOL_CHUNK = 512
HALO = 8


def _cparams(sem):
    return pltpu.CompilerParams(dimension_semantics=sem, vmem_limit_bytes=VMEM_LIMIT)


def _dot(a, b):
    return jnp.dot(a, b, preferred_element_type=f32)


def _dot_nt(a, b):
    return lax.dot_general(a, b, (((1,), (1,)), ((), ())), preferred_element_type=f32)


def _rms(x, gain):
    return x * lax.rsqrt(jnp.mean(x * x, axis=-1, keepdims=True) + EPS) * gain


def _bits(x_f32):
    return lax.bitcast_convert_type(x_f32, u32)


def _unbits(x_u32):
    return lax.bitcast_convert_type(x_u32, f32)


def _pack_store(dst_ref, vals_bf16, rows):
    bits = _bits(vals_bf16.astype(f32))
    for c in range(SLAB):
        lo = bits[:, c * LANES:(c + 1) * LANES] >> 16
        hi = bits[:, HALF + c * LANES:HALF + (c + 1) * LANES] & jnp.uint32(0xFFFF0000)
        dst_ref[pl.ds(c, rows, stride=SLAB), :] = hi | lo


def _unpack_chunk(words):
    lo = _unbits(words << 16)
    hi = _unbits(words & jnp.uint32(0xFFFF0000))
    return lo, hi


def _inproj_kernel(x_ref, g1_ref, w_ref, pos_ref, invf_ref, qg_ref, kg_ref, mg_ref,
                   pool_ref, q_ref, k_ref, v_ref, mq_ref):
    x = x_ref[...]
    xn = _rms(x, g1_ref[...]).astype(bf16)
    ang = pos_ref[...].astype(f32) * invf_ref[...]
    cos = jnp.cos(ang)
    sin = jnp.sin(ang)
    lane = lax.broadcasted_iota(i32, ang.shape, 1)
    sin_s = jnp.where(lane < HD // 2, -sin, sin)

    def rope(u):
        return u * cos + pltpu.roll(u, HD // 2, 1) * sin_s

    pool_ref[...] = _dot(xn, w_ref[:, 0:POOL_W])
    o0 = POOL_W
    uq = _dot(xn, w_ref[:, o0:o0 + QW])
    for h in range(QW // HD):
        q_ref[:, h * HD:(h + 1) * HD] = rope(_rms(uq[:, h * HD:(h + 1) * HD], qg_ref[...])).astype(bf16)
    o0 += QW
    uk = _dot(xn, w_ref[:, o0:o0 + KVW])
    for h in range(KVW // HD):
        k_ref[:, h * HD:(h + 1) * HD] = rope(_rms(uk[:, h * HD:(h + 1) * HD], kg_ref[...])).astype(bf16)
    o0 += KVW
    v_ref[...] = _dot(xn, w_ref[:, o0:o0 + KVW]).astype(bf16)
    o0 += KVW
    um = _dot(xn, w_ref[:, o0:o0 + MQW])
    for h in range(MQW // HD):
        mq_ref[:, h * HD:(h + 1) * HD] = _rms(um[:, h * HD:(h + 1) * HD], mg_ref[...]).astype(bf16)


def _inproj(x2, g1, w_in_bf, pos_col, invf, qg, kg, mg):
    n = x2.shape[0]
    row = lambda i: (i, 0)
    const = lambda i: (0, 0)
    return pl.pallas_call(
        _inproj_kernel,
        grid=(n // TM_IN,),
        in_specs=[
            pl.BlockSpec((TM_IN, D), row),
            pl.BlockSpec((1, D), const),
            pl.BlockSpec((D, A_WIDTH), const),
            pl.BlockSpec((TM_IN, 1), row),
            pl.BlockSpec((1, HD), const),
            pl.BlockSpec((1, HD), const),
            pl.BlockSpec((1, HD), const),
            pl.BlockSpec((1, HD), const),
        ],
        out_specs=[
            pl.BlockSpec((TM_IN, POOL_W), row),
            pl.BlockSpec((TM_IN, QW), row),
            pl.BlockSpec((TM_IN, KVW), row),
            pl.BlockSpec((TM_IN, KVW), row),
            pl.BlockSpec((TM_IN, MQW), row),
        ],
        out_shape=[
            jax.ShapeDtypeStruct((n, POOL_W), f32),
            jax.ShapeDtypeStruct((n, QW), bf16),
            jax.ShapeDtypeStruct((n, KVW), bf16),
            jax.ShapeDtypeStruct((n, KVW), bf16),
            jax.ShapeDtypeStruct((n, MQW), bf16),
        ],
        compiler_params=_cparams(("arbitrary",)),
        name="inproj",
    )(x2, g1, w_in_bf, pos_col, invf, qg, kg, mg)


def _pool_kernel(u_ref, wg_ref, sc_ref, a_ref, pad_ref, *, seq):
    zeros = jnp.zeros((HALO, POOL_W), f32)
    pad_ref[0:HALO, :] = zeros
    pad_ref[seq + HALO:seq + 2 * HALO, :] = zeros
    pad_ref[HALO:seq + HALO, :] = u_ref[0]

    def chunk(ci, carry):
        c0 = pl.multiple_of(ci * POOL_CHUNK, POOL_CHUNK)
        pos = c0 + lax.broadcasted_iota(i32, (POOL_CHUNK, 1), 0)
        for g, w in enumerate(POOL_WINDOWS):
            ext = pad_ref[pl.ds(c0, POOL_CHUNK + 2 * HALO), g * HD:(g + 1) * HD]
            tok = ext[HALO:HALO + POOL_CHUNK]
            win = None
            for o in range(-(w // 2), w // 2):
                term = ext[HALO + o:HALO + o + POOL_CHUNK]
                win = term if win is None else win + term
            lo = jnp.clip(pos - w // 2, 0, seq)
            hi = jnp.clip(pos - w // 2 + w, 0, seq)
            cnt = (hi - lo).astype(f32)
            pooled = (win / cnt - tok).astype(bf16)
            a = _dot(pooled, wg_ref[g]) * sc_ref[:, g * HD:(g + 1) * HD]
            a_ref[0, pl.ds(c0, POOL_CHUNK), g * HD:(g + 1) * HD] = a.astype(bf16)
        return carry

    lax.fori_loop(0, seq // POOL_CHUNK, chunk, 0)


def _pool(u3, wg_bf, scale):
    b, seq, _ = u3.shape
    return pl.pallas_call(
        functools.partial(_pool_kernel, seq=seq),
        grid=(b,),
        in_specs=[
            pl.BlockSpec((1, seq, POOL_W), lambda i: (i, 0, 0)),
            pl.BlockSpec((len(POOL_WINDOWS), HD, HD), lambda i: (0, 0, 0)),
            pl.BlockSpec((1, POOL_W), lambda i: (0, 0)),
        ],
        out_specs=pl.BlockSpec((1, seq, POOL_W), lambda i: (i, 0, 0)),
        out_shape=jax.ShapeDtypeStruct((b, seq, POOL_W), bf16),
        scratch_shapes=[pltpu.VMEM((seq + 2 * HALO, POOL_W), f32)],
        compiler_params=_cparams(("arbitrary",)),
        name="pool",
    )(u3, wg_bf, scale)


def _memkv_kernel(m_ref, g_ref, w_ref, kg_ref, mk_ref, mv_ref):
    mn = _rms(m_ref[...], g_ref[...]).astype(bf16)
    kv = _dot(mn, w_ref[...])
    for h in range(MQW // HD):
        mk_ref[:, h * HD:(h + 1) * HD] = _rms(kv[:, h * HD:(h + 1) * HD], kg_ref[...]).astype(bf16)
    mv_ref[...] = kv[:, MQW:].astype(bf16)


def _memkv(mem2, gain, w_bf, kg, n_mem):
    rows = mem2.shape[0]
    return pl.pallas_call(
        _memkv_kernel,
        grid=(rows // n_mem,),
        in_specs=[
            pl.BlockSpec((n_mem, D), lambda i: (i, 0)),
            pl.BlockSpec((1, D), lambda i: (0, 0)),
            pl.BlockSpec((D, 2 * MQW), lambda i: (0, 0)),
            pl.BlockSpec((1, HD), lambda i: (0, 0)),
        ],
        out_specs=[pl.BlockSpec((n_mem, MQW), lambda i: (i, 0))] * 2,
        out_shape=[jax.ShapeDtypeStruct((rows, MQW), bf16)] * 2,
        compiler_params=_cparams(("arbitrary",)),
        name="memkv",
    )(mem2, gain, w_bf, kg)


def _attn_kernel(sink_ref, q_ref, kp_ref, kc_ref, kn_ref, vp_ref, vc_ref, vn_ref, mq_ref, mk_ref, mv_ref,
                 ao_ref, mo_ref, *, seq):
    blk = WINDOW
    nb = pl.program_id(1)
    scale = 1.0 / np.sqrt(HD)
    group = (QW // HD) // (KVW // HD)
    rows = group * blk

    r = lax.broadcasted_iota(i32, (rows, 3 * blk), 0)
    c = lax.broadcasted_iota(i32, (rows, 3 * blk), 1)
    qi = (r % blk) + blk
    key_abs = (nb - 1) * blk + c
    mask = (jnp.abs(c - qi) <= WINDOW) & (key_abs >= 0) & (key_abs < seq)
    rcol = lax.broadcasted_iota(i32, (rows, 1), 0) // blk

    for kh in range(KVW // HD):
        ks = slice(kh * HD, (kh + 1) * HD)
        kb = jnp.concatenate([kp_ref[:, ks], kc_ref[:, ks], kn_ref[:, ks]], axis=0)
        vb = jnp.concatenate([vp_ref[:, ks], vc_ref[:, ks], vn_ref[:, ks]], axis=0)
        qh = jnp.concatenate(
            [q_ref[:, (kh * group + g) * HD:(kh * group + g + 1) * HD] for g in range(group)], axis=0)
        s = _dot_nt(qh, kb) * scale
        s = jnp.where(mask, s, NEG)
        sk = jnp.zeros((rows, 1), f32)
        for g in range(group):
            sk = jnp.where(rcol == g, sink_ref[kh * group + g], sk)
        m = jnp.maximum(jnp.max(s, axis=-1, keepdims=True), sk)
        p = jnp.exp(s - m)
        denom = jnp.sum(p, axis=-1, keepdims=True) + jnp.exp(sk - m)
        probs = (p / denom).astype(bf16)
        o = _dot(probs, vb)
        for g in range(group):
            h = kh * group + g
            ao_ref[:, h * HD:(h + 1) * HD] = o[g * blk:(g + 1) * blk].astype(bf16)

    for h in range(MQW // HD):
        hs = slice(h * HD, (h + 1) * HD)
        s = _dot_nt(mq_ref[:, hs], mk_ref[:, hs]) * scale
        m = jnp.max(s, axis=-1, keepdims=True)
        p = jnp.exp(s - m)
        probs = (p / jnp.sum(p, axis=-1, keepdims=True)).astype(bf16)
        mo_ref[:, hs] = _dot(probs, mv_ref[:, hs]).astype(bf16)


def _attn(sink, q, k, v, mq, mk, mv, bsz, seq, n_mem):
    nblk = seq // WINDOW
    n = bsz * seq
    cur = lambda b, j: (b * nblk + j, 0)
    prev = lambda b, j: (b * nblk + jnp.maximum(j - 1, 0), 0)
    nxt = lambda b, j: (b * nblk + jnp.minimum(j + 1, nblk - 1), 0)
    memb = lambda b, j: (b, 0)
    return pl.pallas_call(
        functools.partial(_attn_kernel, seq=seq),
        grid=(bsz, nblk),
        in_specs=[
            pl.BlockSpec(memory_space=pltpu.SMEM),
            pl.BlockSpec((WINDOW, QW), cur),
            pl.BlockSpec((WINDOW, KVW), prev),
            pl.BlockSpec((WINDOW, KVW), cur),
            pl.BlockSpec((WINDOW, KVW), nxt),
            pl.BlockSpec((WINDOW, KVW), prev),
            pl.BlockSpec((WINDOW, KVW), cur),
            pl.BlockSpec((WINDOW, KVW), nxt),
            pl.BlockSpec((WINDOW, MQW), cur),
            pl.BlockSpec((n_mem, MQW), memb),
            pl.BlockSpec((n_mem, MQW), memb),
        ],
        out_specs=[pl.BlockSpec((WINDOW, QW), cur), pl.BlockSpec((WINDOW, MQW), cur)],
        out_shape=[jax.ShapeDtypeStruct((n, QW), bf16), jax.ShapeDtypeStruct((n, MQW), bf16)],
        compiler_params=_cparams(("arbitrary", "arbitrary")),
        name="attn",
    )(sink, q, k, k, k, v, v, v, mq, mk, mv)


def _merge_kernel(x_ref, g1_ref, a_ref, ao_ref, mo_ref, wpp_ref, wap_ref, wmp_ref,
                  wg0_ref, wg1_ref, wg2_ref, bg0_ref, bg1_ref, bg2_ref, wout_ref,
                  g2_ref, wrh_ref, wrl_ref, br_ref,
                  h_ref, x32_ref, lg_ref, xn_ref):
    j = pl.program_id(1)

    @pl.when(j == 0)
    def _():
        x = x_ref[...]
        xn_ref[...] = _rms(x, g1_ref[...]).astype(bf16)
        h_ref[...] = x

    xn = xn_ref[...]
    gate0 = jax.nn.sigmoid(_dot(xn, wg0_ref[...]) + bg0_ref[...])
    gate1 = jax.nn.sigmoid(_dot(xn, wg1_ref[...]) + bg1_ref[...])
    gate2 = jax.nn.sigmoid(_dot(xn, wg2_ref[...]) + bg2_ref[...])
    merged = (gate0 * _dot(a_ref[...], wpp_ref[...])
              + gate1 * _dot(ao_ref[...], wap_ref[...])
              + gate2 * _dot(mo_ref[...], wmp_ref[...]))
    h_ref[...] += _dot(merged.astype(bf16), wout_ref[...])

    @pl.when(j == pl.num_programs(1) - 1)
    def _():
        xn2 = _rms(h_ref[...], g2_ref[...])
        hi = xn2.astype(bf16)
        lo = (xn2 - hi.astype(f32)).astype(bf16)
        lg_ref[...] = (_dot_nt(wrh_ref[...], hi) + _dot_nt(wrl_ref[...], hi)
                       + _dot_nt(wrh_ref[...], lo) + br_ref[...])
        _pack_store(x32_ref, hi, TM_MG)


def _merge(x2, g1, a, ao, mo, wpp, wap, wmp, w_in_bf, b_gate, wout, g2, wr_hi, wr_lo, br_col):
    n = x2.shape[0]
    nt = D // TN_MG
    goff = A_WIDTH // TN_MG
    row = lambda i, j: (i, 0)
    col = lambda i, j: (0, j)
    const = lambda i, j: (0, 0)
    return pl.pallas_call(
        _merge_kernel,
        grid=(n // TM_MG, nt),
        in_specs=[
            pl.BlockSpec((TM_MG, D), row),
            pl.BlockSpec((1, D), const),
            pl.BlockSpec((TM_MG, POOL_W), row),
            pl.BlockSpec((TM_MG, QW), row),
            pl.BlockSpec((TM_MG, MQW), row),
            pl.BlockSpec((POOL_W, TN_MG), col),
            pl.BlockSpec((QW, TN_MG), col),
            pl.BlockSpec((MQW, TN_MG), col),
            pl.BlockSpec((D, TN_MG), lambda i, j: (0, goff + j)),
            pl.BlockSpec((D, TN_MG), lambda i, j: (0, goff + nt + j)),
            pl.BlockSpec((D, TN_MG), lambda i, j: (0, goff + 2 * nt + j)),
            pl.BlockSpec((1, TN_MG), lambda i, j: (0, j)),
            pl.BlockSpec((1, TN_MG), lambda i, j: (0, nt + j)),
            pl.BlockSpec((1, TN_MG), lambda i, j: (0, 2 * nt + j)),
            pl.BlockSpec((TN_MG, D), lambda i, j: (j, 0)),
            pl.BlockSpec((1, D), const),
            pl.BlockSpec((N_EXP, D), const),
            pl.BlockSpec((N_EXP, D), const),
            pl.BlockSpec((N_EXP, 1), const),
        ],
        out_specs=[
            pl.BlockSpec((TM_MG, D), row),
            pl.BlockSpec((TM_MG * SLAB, LANES), row),
            pl.BlockSpec((N_EXP, TM_MG), lambda i, j: (0, i)),
        ],
        out_shape=[
            jax.ShapeDtypeStruct((n, D), f32),
            jax.ShapeDtypeStruct((n * SLAB, LANES), u32),
            jax.ShapeDtypeStruct((N_EXP, n), f32),
        ],
        scratch_shapes=[pltpu.VMEM((TM_MG, D), bf16)],
        compiler_params=_cparams(("arbitrary", "arbitrary")),
        name="merge",
    )(x2, g1, a, ao, mo, wpp, wap, wmp, w_in_bf, w_in_bf, w_in_bf, b_gate, b_gate, b_gate, wout,
      g2, wr_hi, wr_lo, br_col)


def _route_kernel(lg_ref, tri_ref, dest_ref, w_ref, cnt_ref, cnt_scr, carry_scr):
    p = pl.program_id(0)
    i = pl.program_id(1)

    @pl.when((p == 0) & (i == 0))
    def _():
        cnt_scr[...] = jnp.zeros_like(cnt_scr)

    @pl.when(i == 0)
    def _():
        carry_scr[...] = jnp.zeros_like(carry_scr)

    l = lg_ref[...]
    eidx = lax.broadcasted_iota(i32, l.shape, 0)
    sels, tops = [], []
    for _ in range(TOP_K):
        m = jnp.max(l, axis=0, keepdims=True)
        idx = jnp.min(jnp.where(l == m, eidx, N_EXP), axis=0, keepdims=True)
        sel = eidx == idx
        l = jnp.where(sel, -jnp.inf, l)
        sels.append(sel)
        tops.append(m)
    es = [jnp.exp(t - tops[0]) for t in tops]
    tot = es[0] + es[1] + es[2] + es[3]
    w_ref[...] = jnp.concatenate([e / tot for e in es], axis=0)

    onehot = jnp.zeros(l.shape, f32)
    for sel in sels:
        onehot = jnp.where(sel, 1.0, onehot)
    blk_cnt = jnp.sum(onehot, axis=1, keepdims=True)

    @pl.when(p == 0)
    def _():
        cnt_scr[...] += blk_cnt

    cnt = cnt_scr[...][:, 0:1].astype(i32)
    padded = ((cnt + (MOE_BLOCK - 1)) // MOE_BLOCK * MOE_BLOCK).astype(f32)
    rr = lax.broadcasted_iota(i32, (N_EXP, N_EXP), 0)
    cc = lax.broadcasted_iota(i32, (N_EXP, N_EXP), 1)
    start_row = jnp.sum(jnp.where(rr < cc, padded, 0.0), axis=0, keepdims=True)
    start_col = jnp.sum(jnp.where(rr == cc, start_row, 0.0), axis=1, keepdims=True)
    cum = _dot(onehot.astype(bf16), tri_ref[...])
    val = cum + start_col + carry_scr[...][:, 0:1]
    dest_ref[...] = jnp.concatenate(
        [jnp.sum(jnp.where(sel, val, 0.0), axis=0, keepdims=True) for sel in sels], axis=0).astype(i32)
    carry_scr[...] += blk_cnt
    cnt_ref[...] = cnt_scr[...]


def _route(logits_t, tri):
    n = logits_t.shape[1]
    blk = lambda p, i: (0, i)
    final = lambda p, i: (0, i * p)
    return pl.pallas_call(
        _route_kernel,
        grid=(2, n // TR),
        in_specs=[pl.BlockSpec((N_EXP, TR), blk), pl.BlockSpec((TR, TR), lambda p, i: (0, 0))],
        out_specs=[
            pl.BlockSpec((TOP_K, TR), final),
            pl.BlockSpec((TOP_K, TR), final),
            pl.BlockSpec((N_EXP, LANES), lambda p, i: (0, 0)),
        ],
        out_shape=[
            jax.ShapeDtypeStruct((TOP_K, n), i32),
            jax.ShapeDtypeStruct((TOP_K, n), f32),
            jax.ShapeDtypeStruct((N_EXP, LANES), f32),
        ],
        scratch_shapes=[pltpu.VMEM((N_EXP, LANES), f32), pltpu.VMEM((N_EXP, LANES), f32)],
        compiler_params=_cparams(("arbitrary", "arbitrary")),
        name="route",
    )(logits_t, tri)


def _dispatch_kernel(dest_ref, x_ref, init_hbm, xs_hbm, sem, *, n_tok):
    del init_hbm
    base = pl.program_id(0) * TD

    def body(t, carry):
        src = x_ref.at[pl.ds(pl.multiple_of(t * SLAB, SLAB), SLAB)]
        for k in range(TOP_K):
            d = dest_ref[k * n_tok + base + t]
            pltpu.make_async_copy(src, xs_hbm.at[pl.ds(pl.multiple_of(d * SLAB, SLAB), SLAB)], sem).start()
        return carry

    lax.fori_loop(0, TD, body, 0)
    rows = TOP_K * TD * SLAB
    pltpu.make_async_copy(xs_hbm.at[pl.ds(0, rows)], xs_hbm.at[pl.ds(0, rows)], sem).wait()


def _dispatch(dest_flat, x32, xs_init, n_tok):
    return pl.pallas_call(
        functools.partial(_dispatch_kernel, n_tok=n_tok),
        grid_spec=pltpu.PrefetchScalarGridSpec(
            num_scalar_prefetch=1,
            grid=(n_tok // TD,),
            in_specs=[pl.BlockSpec((TD * SLAB, LANES), lambda i, d: (i, 0)), pl.BlockSpec(memory_space=pl.ANY)],
            out_specs=pl.BlockSpec(memory_space=pl.ANY),
            scratch_shapes=[pltpu.SemaphoreType.DMA(())],
        ),
        out_shape=jax.ShapeDtypeStruct(xs_init.shape, u32),
        input_output_aliases={2: 0},
        compiler_params=_cparams(("arbitrary",)),
        name="dispatch",
    )(dest_flat, x32, xs_init)


CAST_ROWS = 256


def _cast_weights(w_ref, wbf_ref):
    def body(r, carry):
        r0 = pl.multiple_of(r * CAST_ROWS, CAST_ROWS)
        wbf_ref[pl.ds(r0, CAST_ROWS), :] = w_ref[pl.ds(r0, CAST_ROWS), :].astype(bf16)
        return carry

    lax.fori_loop(0, w_ref.shape[0] // CAST_ROWS, body, 0)


def _expert_changed(be_ref, i):
    return (i == 0) | (be_ref[i] != be_ref[jnp.maximum(i - 1, 0)])


def _moe_up_kernel(be_ref, nb_ref, x_ref, w_ref, b_ref, sel_ref, act_ref, wbf_ref):
    i = pl.program_id(1)

    @pl.when(_expert_changed(be_ref, i))
    def _():
        _cast_weights(w_ref, wbf_ref)

    @pl.when(i < nb_ref[0])
    def _():
        lo_hi = [_unpack_chunk(x_ref[pl.ds(c, MOE_BLOCK, stride=SLAB), :]) for c in range(SLAB)]
        x = jnp.concatenate([p[0].astype(bf16) for p in lo_hi] + [p[1].astype(bf16) for p in lo_hi], axis=1)
        n_chunks = TN_UP // UP_CHUNK

        def pre_act(n):
            ns = slice(n * UP_CHUNK, (n + 1) * UP_CHUNK)
            return _dot(x, wbf_ref[:, ns]) + b_ref[:, ns]

        hb_next = pre_act(0)
        for n in range(n_chunks):
            hb = hb_next
            if n + 1 < n_chunks:
                hb_next = pre_act(n + 1)
            gate = jnp.minimum(hb, LIMIT)
            up = jnp.clip(hb, -LIMIT, LIMIT)
            up_at_even = pltpu.roll(up, UP_CHUNK - 1, 1)
            act = (gate * jax.nn.sigmoid(ALPHA * gate) * (up_at_even + 1.0)).astype(bf16)
            for c in range(UP_CHUNK // (2 * LANES)):
                o0 = n * (UP_CHUNK // 2) + c * LANES
                act_ref[:, o0:o0 + LANES] = _dot(
                    act[:, c * 2 * LANES:(c + 1) * 2 * LANES], sel_ref[...]).astype(bf16)

    @pl.when(i >= nb_ref[0])
    def _():
        act_ref[...] = jnp.zeros_like(act_ref)


def _moe_up(blk_expert, nb_used, xs, w_up, b_up3, sel, cap):
    n_blocks = cap // MOE_BLOCK
    return pl.pallas_call(
        _moe_up_kernel,
        grid_spec=pltpu.PrefetchScalarGridSpec(
            num_scalar_prefetch=2,
            grid=(2 * D_FF // TN_UP, n_blocks),
            in_specs=[
                pl.BlockSpec((MOE_BLOCK * SLAB, LANES), lambda j, i, be, nb: (i, 0)),
                pl.BlockSpec((None, D, TN_UP), lambda j, i, be, nb: (be[i], 0, j)),
                pl.BlockSpec((None, 1, TN_UP), lambda j, i, be, nb: (be[i], 0, j)),
                pl.BlockSpec((2 * LANES, LANES), lambda j, i, be, nb: (0, 0)),
            ],
            out_specs=pl.BlockSpec((MOE_BLOCK, TN_UP // 2), lambda j, i, be, nb: (i, j)),
            scratch_shapes=[pltpu.VMEM((D, TN_UP), bf16)],
        ),
        out_shape=jax.ShapeDtypeStruct((cap, D_FF), bf16),
        compiler_params=_cparams(("arbitrary", "arbitrary")),
        name="moe_up",
    )(blk_expert, nb_used, xs, w_up, b_up3, sel)


def _moe_down_kernel(be_ref, nb_ref, a_ref, w_ref, b_ref, y_ref, wbf_ref):
    i = pl.program_id(0)

    @pl.when(_expert_changed(be_ref, i))
    def _():
        _cast_weights(w_ref, wbf_ref)

    @pl.when(i < nb_ref[0])
    def _():
        y = _dot(a_ref[...], wbf_ref[...]) + b_ref[...]
        _pack_store(y_ref, y.astype(bf16), MOE_BLOCK)

    @pl.when(i >= nb_ref[0])
    def _():
        y_ref[...] = jnp.zeros_like(y_ref)


def _moe_down(blk_expert, nb_used, act, w_down, b_down3, cap):
    n_blocks = cap // MOE_BLOCK
    return pl.pallas_call(
        _moe_down_kernel,
        grid_spec=pltpu.PrefetchScalarGridSpec(
            num_scalar_prefetch=2,
            grid=(n_blocks,),
            in_specs=[
                pl.BlockSpec((MOE_BLOCK, D_FF), lambda i, be, nb: (i, 0)),
                pl.BlockSpec((None, D_FF, D), lambda i, be, nb: (be[i], 0, 0)),
                pl.BlockSpec((None, 1, D), lambda i, be, nb: (be[i], 0, 0)),
            ],
            out_specs=pl.BlockSpec((MOE_BLOCK * SLAB, LANES), lambda i, be, nb: (i, 0)),
            scratch_shapes=[pltpu.VMEM((D_FF, D), bf16)],
        ),
        out_shape=jax.ShapeDtypeStruct((cap * SLAB, LANES), u32),
        compiler_params=_cparams(("arbitrary",)),
        name="moe_down",
    )(blk_expert, nb_used, act, w_down, b_down3)


def _combine_kernel(dest_ref, y_hbm, h_ref, w_ref, o_ref, buf0, buf1, sem0, sem1, *, n_tok):
    i = pl.program_id(0)

    def issue(tb, buf, sem):
        base = tb * TC

        def body(t, carry):
            for k in range(TOP_K):
                d = dest_ref[k * n_tok + base + t]
                pltpu.make_async_copy(
                    y_hbm.at[pl.ds(pl.multiple_of(d * SLAB, SLAB), SLAB)],
                    buf.at[pl.ds(pl.multiple_of((k * TC + t) * SLAB, SLAB), SLAB)], sem).start()
            return carry

        lax.fori_loop(0, TC, body, 0)

    def wait(buf, sem):
        pltpu.make_async_copy(y_hbm.at[pl.ds(0, TOP_K * TC * SLAB)], buf, sem).wait()

    def compute(half, buf):
        rows = slice(half * TC, (half + 1) * TC)
        wv = w_ref[rows, :]
        for c in range(SLAB):
            cl = slice(c * LANES, (c + 1) * LANES)
            ch = slice(HALF + c * LANES, HALF + (c + 1) * LANES)
            acc_lo = h_ref[rows, cl]
            acc_hi = h_ref[rows, ch]
            for k in range(TOP_K):
                lo, hi = _unpack_chunk(buf[pl.ds(k * TC * SLAB + c, TC, stride=SLAB), :])
                acc_lo = acc_lo + wv[:, k:k + 1] * lo
                acc_hi = acc_hi + wv[:, k:k + 1] * hi
            o_ref[rows, cl] = acc_lo
            o_ref[rows, ch] = acc_hi

    @pl.when(i == 0)
    def _():
        issue(0, buf0, sem0)

    issue(2 * i + 1, buf1, sem1)
    wait(buf0, sem0)
    compute(0, buf0)

    @pl.when(i + 1 < pl.num_programs(0))
    def _():
        issue(2 * i + 2, buf0, sem0)

    wait(buf1, sem1)
    compute(1, buf1)


def _combine(dest_flat, y32, h, w_tok, n_tok):
    return pl.pallas_call(
        functools.partial(_combine_kernel, n_tok=n_tok),
        grid_spec=pltpu.PrefetchScalarGridSpec(
            num_scalar_prefetch=1,
            grid=(n_tok // (2 * TC),),
            in_specs=[
                pl.BlockSpec(memory_space=pl.ANY),
                pl.BlockSpec((2 * TC, D), lambda i, d: (i, 0)),
                pl.BlockSpec((2 * TC, TOP_K), lambda i, d: (i, 0)),
            ],
            out_specs=pl.BlockSpec((2 * TC, D), lambda i, d: (i, 0)),
            scratch_shapes=[
                pltpu.VMEM((TOP_K * TC * SLAB, LANES), u32),
                pltpu.VMEM((TOP_K * TC * SLAB, LANES), u32),
                pltpu.SemaphoreType.DMA(()),
                pltpu.SemaphoreType.DMA(()),
            ],
        ),
        out_shape=jax.ShapeDtypeStruct((n_tok, D), f32),
        compiler_params=_cparams(("arbitrary",)),
        name="combine",
    )(dest_flat, y32, h, w_tok)


def kernel(x, mem, positions, norm1_gain, w_in, b_gate, w_pool_group, pool_scale, w_pool_proj, attn_q_norm, attn_k_norm, attn_sink, w_attn_proj, mem_norm_gain, w_mem_kv, mem_q_norm, mem_k_norm, w_mem_proj, w_out, norm2_gain, w_router, b_router, w_up, b_up, w_down, b_down):
    bsz, seq, d = x.shape
    n_mem = mem.shape[1]
    n_tok = bsz * seq
    n_assign = n_tok * TOP_K
    n_blocks = -(-n_assign // MOE_BLOCK) + N_EXP
    cap = n_blocks * MOE_BLOCK
    depth = norm1_gain.shape[0]

    half = HD // 2
    inv_freq = jnp.power(jnp.float32(THETA), -jnp.arange(half, dtype=f32) * (2.0 / HD))
    invf = jnp.concatenate([inv_freq, inv_freq])[None, :]
    pos_col = positions.reshape(n_tok, 1)
    tri = jnp.asarray(np.triu(np.ones((TR, TR), np.float32), 1), bf16)
    sel_np = np.zeros((2 * LANES, LANES), np.float32)
    sel_np[2 * np.arange(LANES), np.arange(LANES)] = 1.0
    sel = jnp.asarray(sel_np, bf16)

    h2 = x.reshape(n_tok, d)
    mem2 = mem.reshape(bsz * n_mem, d)
    for l in range(depth):
        row = lambda v: v[l][None, :]
        w_in_bf = w_in[l].astype(bf16)
        pool_u, q, k, v, mq = _inproj(h2, row(norm1_gain), w_in_bf, pos_col, invf,
                                      row(attn_q_norm), row(attn_k_norm), row(mem_q_norm))
        a = _pool(pool_u.reshape(bsz, seq, POOL_W), w_pool_group[l].astype(bf16), row(pool_scale))
        mk, mv = _memkv(mem2, row(mem_norm_gain), w_mem_kv[l].astype(bf16), row(mem_k_norm), n_mem)
        ao, mo = _attn(attn_sink[l], q, k, v, mq, mk, mv, bsz, seq, n_mem)
        wr_t = w_router[l].T
        wr_hi = wr_t.astype(bf16)
        wr_lo = (wr_t - wr_hi.astype(f32)).astype(bf16)
        hmid, x32, logits_t = _merge(
            h2, row(norm1_gain), a.reshape(n_tok, POOL_W), ao, mo,
            w_pool_proj[l].astype(bf16), w_attn_proj[l].astype(bf16), w_mem_proj[l].astype(bf16),
            w_in_bf, row(b_gate), w_out[l].astype(bf16), row(norm2_gain), wr_hi, wr_lo,
            b_router[l][:, None])

        dest, w_top, cnt = _route(logits_t, tri)
        counts = cnt[:, 0].astype(i32)
        padded = (counts + MOE_BLOCK - 1) // MOE_BLOCK * MOE_BLOCK
        pad_ends = jnp.cumsum(padded)
        blk_start = jnp.arange(n_blocks, dtype=i32) * MOE_BLOCK
        blk_expert = jnp.minimum(jnp.sum((pad_ends[None, :] <= blk_start[:, None]).astype(i32), axis=1),
                                 N_EXP - 1)
        nb_used = (pad_ends[-1:] // MOE_BLOCK).astype(i32)
        dest_flat = dest.reshape(-1)

        xs32 = _dispatch(dest_flat, x32, jnp.zeros((cap * SLAB, LANES), u32), n_tok)
        act = _moe_up(blk_expert, nb_used, xs32, w_up[l], b_up[l][:, None, :], sel, cap)
        y32 = _moe_down(blk_expert, nb_used, act, w_down[l], b_down[l][:, None, :], cap)
        h2 = _combine(dest_flat, y32, hmid, w_top.T, n_tok)
    return h2.reshape(bsz, seq, d)
```

```python
import functools

import numpy as np
import jax
import jax.numpy as jnp
from jax import lax
from jax.experimental import pallas as pl
from jax.experimental.pallas import tpu as pltpu

f32 = jnp.float32
bf16 = jnp.bfloat16
i32 = jnp.int32
u32 = jnp.uint32

D = 2048
HD = 128
POOL_WINDOWS = (2, 4, 8, 16)
POOL_W = 512
QW = 1024
KVW = 256
MQW = 512
A_WIDTH = POOL_W + QW + 2 * KVW + MQW
N_EXP = 32
TOP_K = 4
D_FF = 2048
LIMIT = 7.0
ALPHA = 1.702
MOE_BLOCK = 512
HALF_BLOCK = MOE_BLOCK // 2
EPS = 1e-6
NEG = -1e30
THETA = 10000.0
WINDOW = 128

LANES = 128
SLAB = 8
HALF = D // 2
VMEM_LIMIT = 56 * 1024 * 1024

TM_IN = 512
TM_MG = 512
TN_MG = 256
TR = 512
TD = 512
TC = 128
TN_UP = 2048
UP_CHUNK = 512
POOL_CHUNK = 512
HALO = 8


def _cparams(sem):
    return pltpu.CompilerParams(dimension_semantics=sem, vmem_limit_bytes=VMEM_LIMIT)


def _dot(a, b):
    return jnp.dot(a, b, preferred_element_type=f32)


def _dot_nt(a, b):
    return lax.dot_general(a, b, (((1,), (1,)), ((), ())), preferred_element_type=f32)


def _rms(x, gain):
    return x * lax.rsqrt(jnp.mean(x * x, axis=-1, keepdims=True) + EPS) * gain


def _bits(x_f32):
    return lax.bitcast_convert_type(x_f32, u32)


def _unbits(x_u32):
    return lax.bitcast_convert_type(x_u32, f32)


def _pack_store(dst_ref, vals_bf16, rows):
    bits = _bits(vals_bf16.astype(f32))
    for c in range(SLAB):
        lo = bits[:, c * LANES:(c + 1) * LANES] >> 16
        hi = bits[:, HALF + c * LANES:HALF + (c + 1) * LANES] & jnp.uint32(0xFFFF0000)
        dst_ref[pl.ds(c, rows, stride=SLAB), :] = hi | lo


def _unpack_chunk(words):
    lo = _unbits(words << 16)
    hi = _unbits(words & jnp.uint32(0xFFFF0000))
    return lo, hi


def _inproj_kernel(x_ref, g1_ref, w_ref, pos_ref, invf_ref, qg_ref, kg_ref, mg_ref,
                   pool_ref, q_ref, k_ref, v_ref, mq_ref):
    x = x_ref[...]
    xn = _rms(x, g1_ref[...]).astype(bf16)
    ang = pos_ref[...].astype(f32) * invf_ref[...]
    cos = jnp.cos(ang)
    sin = jnp.sin(ang)
    lane = lax.broadcasted_iota(i32, ang.shape, 1)
    sin_s = jnp.where(lane < HD // 2, -sin, sin)

    def rope(u):
        return u * cos + pltpu.roll(u, HD // 2, 1) * sin_s

    pool_ref[...] = _dot(xn, w_ref[:, 0:POOL_W])
    o0 = POOL_W
    uq = _dot(xn, w_ref[:, o0:o0 + QW])
    for h in range(QW // HD):
        q_ref[:, h * HD:(h + 1) * HD] = rope(_rms(uq[:, h * HD:(h + 1) * HD], qg_ref[...])).astype(bf16)
    o0 += QW
    uk = _dot(xn, w_ref[:, o0:o0 + KVW])
    for h in range(KVW // HD):
        k_ref[:, h * HD:(h + 1) * HD] = rope(_rms(uk[:, h * HD:(h + 1) * HD], kg_ref[...])).astype(bf16)
    o0 += KVW
    v_ref[...] = _dot(xn, w_ref[:, o0:o0 + KVW]).astype(bf16)
    o0 += KVW
    um = _dot(xn, w_ref[:, o0:o0 + MQW])
    for h in range(MQW // HD):
        mq_ref[:, h * HD:(h + 1) * HD] = _rms(um[:, h * HD:(h + 1) * HD], mg_ref[...]).astype(bf16)


def _inproj(x2, g1, w_in_bf, pos_col, invf, qg, kg, mg):
    n = x2.shape[0]
    row = lambda i: (i, 0)
    const = lambda i: (0, 0)
    return pl.pallas_call(
        _inproj_kernel,
        grid=(n // TM_IN,),
        in_specs=[
            pl.BlockSpec((TM_IN, D), row),
            pl.BlockSpec((1, D), const),
            pl.BlockSpec((D, A_WIDTH), const),
            pl.BlockSpec((TM_IN, 1), row),
            pl.BlockSpec((1, HD), const),
            pl.BlockSpec((1, HD), const),
            pl.BlockSpec((1, HD), const),
            pl.BlockSpec((1, HD), const),
        ],
        out_specs=[
            pl.BlockSpec((TM_IN, POOL_W), row),
            pl.BlockSpec((TM_IN, QW), row),
            pl.BlockSpec((TM_IN, KVW), row),
            pl.BlockSpec((TM_IN, KVW), row),
            pl.BlockSpec((TM_IN, MQW), row),
        ],
        out_shape=[
            jax.ShapeDtypeStruct((n, POOL_W), f32),
            jax.ShapeDtypeStruct((n, QW), bf16),
            jax.ShapeDtypeStruct((n, KVW), bf16),
            jax.ShapeDtypeStruct((n, KVW), bf16),
            jax.ShapeDtypeStruct((n, MQW), bf16),
        ],
        compiler_params=_cparams(("arbitrary",)),
        name="inproj",
    )(x2, g1, w_in_bf, pos_col, invf, qg, kg, mg)


def _pool_kernel(u_ref, wg_ref, sc_ref, a_ref, pad_ref, *, seq):
    zeros = jnp.zeros((HALO, POOL_W), f32)
    pad_ref[0:HALO, :] = zeros
    pad_ref[seq + HALO:seq + 2 * HALO, :] = zeros
    pad_ref[HALO:seq + HALO, :] = u_ref[0]

    def chunk(ci, carry):
        c0 = pl.multiple_of(ci * POOL_CHUNK, POOL_CHUNK)
        pos = c0 + lax.broadcasted_iota(i32, (POOL_CHUNK, 1), 0)
        for g, w in enumerate(POOL_WINDOWS):
            ext = pad_ref[pl.ds(c0, POOL_CHUNK + 2 * HALO), g * HD:(g + 1) * HD]
            tok = ext[HALO:HALO + POOL_CHUNK]
            win = None
            for o in range(-(w // 2), w // 2):
                term = ext[HALO + o:HALO + o + POOL_CHUNK]
                win = term if win is None else win + term
            lo = jnp.clip(pos - w // 2, 0, seq)
            hi = jnp.clip(pos - w // 2 + w, 0, seq)
            cnt = (hi - lo).astype(f32)
            pooled = (win / cnt - tok).astype(bf16)
            a = _dot(pooled, wg_ref[g]) * sc_ref[:, g * HD:(g + 1) * HD]
            a_ref[0, pl.ds(c0, POOL_CHUNK), g * HD:(g + 1) * HD] = a.astype(bf16)
        return carry

    lax.fori_loop(0, seq // POOL_CHUNK, chunk, 0)


def _pool(u3, wg_bf, scale):
    b, seq, _ = u3.shape
    return pl.pallas_call(
        functools.partial(_pool_kernel, seq=seq),
        grid=(b,),
        in_specs=[
            pl.BlockSpec((1, seq, POOL_W), lambda i: (i, 0, 0)),
            pl.BlockSpec((len(POOL_WINDOWS), HD, HD), lambda i: (0, 0, 0)),
            pl.BlockSpec((1, POOL_W), lambda i: (0, 0)),
        ],
        out_specs=pl.BlockSpec((1, seq, POOL_W), lambda i: (i, 0, 0)),
        out_shape=jax.ShapeDtypeStruct((b, seq, POOL_W), bf16),
        scratch_shapes=[pltpu.VMEM((seq + 2 * HALO, POOL_W), f32)],
        compiler_params=_cparams(("arbitrary",)),
        name="pool",
    )(u3, wg_bf, scale)


def _memkv_kernel(m_ref, g_ref, w_ref, kg_ref, mk_ref, mv_ref):
    mn = _rms(m_ref[...], g_ref[...]).astype(bf16)
    kv = _dot(mn, w_ref[...])
    for h in range(MQW // HD):
        mk_ref[:, h * HD:(h + 1) * HD] = _rms(kv[:, h * HD:(h + 1) * HD], kg_ref[...]).astype(bf16)
    mv_ref[...] = kv[:, MQW:].astype(bf16)


def _memkv(mem2, gain, w_bf, kg, n_mem):
    rows = mem2.shape[0]
    return pl.pallas_call(
        _memkv_kernel,
        grid=(rows // n_mem,),
        in_specs=[
            pl.BlockSpec((n_mem, D), lambda i: (i, 0)),
            pl.BlockSpec((1, D), lambda i: (0, 0)),
            pl.BlockSpec((D, 2 * MQW), lambda i: (0, 0)),
            pl.BlockSpec((1, HD), lambda i: (0, 0)),
        ],
        out_specs=[pl.BlockSpec((n_mem, MQW), lambda i: (i, 0))] * 2,
        out_shape=[jax.ShapeDtypeStruct((rows, MQW), bf16)] * 2,
        compiler_params=_cparams(("arbitrary",)),
        name="memkv",
    )(mem2, gain, w_bf, kg)


def _attn_kernel(sink_ref, q_ref, kp_ref, kc_ref, kn_ref, vp_ref, vc_ref, vn_ref, mq_ref, mk_ref, mv_ref,
                 ao_ref, mo_ref, *, seq):
    blk = WINDOW
    nb = pl.program_id(1)
    scale = 1.0 / np.sqrt(HD)
    group = (QW // HD) // (KVW // HD)
    rows = group * blk

    r = lax.broadcasted_iota(i32, (rows, 3 * blk), 0)
    c = lax.broadcasted_iota(i32, (rows, 3 * blk), 1)
    qi = (r % blk) + blk
    key_abs = (nb - 1) * blk + c
    mask = (jnp.abs(c - qi) <= WINDOW) & (key_abs >= 0) & (key_abs < seq)
    rcol = lax.broadcasted_iota(i32, (rows, 1), 0) // blk

    for kh in range(KVW // HD):
        ks = slice(kh * HD, (kh + 1) * HD)
        kb = jnp.concatenate([kp_ref[:, ks], kc_ref[:, ks], kn_ref[:, ks]], axis=0)
        vb = jnp.concatenate([vp_ref[:, ks], vc_ref[:, ks], vn_ref[:, ks]], axis=0)
        qh = jnp.concatenate(
            [q_ref[:, (kh * group + g) * HD:(kh * group + g + 1) * HD] for g in range(group)], axis=0)
        s = _dot_nt(qh, kb) * scale
        s = jnp.where(mask, s, NEG)
        sk = jnp.zeros((rows, 1), f32)
        for g in range(group):
            sk = jnp.where(rcol == g, sink_ref[kh * group + g], sk)
        m = jnp.maximum(jnp.max(s, axis=-1, keepdims=True), sk)
        p = jnp.exp(s - m)
        denom = jnp.sum(p, axis=-1, keepdims=True) + jnp.exp(sk - m)
        probs = (p / denom).astype(bf16)
        o = _dot(probs, vb)
        for g in range(group):
            h = kh * group + g
            ao_ref[:, h * HD:(h + 1) * HD] = o[g * blk:(g + 1) * blk].astype(bf16)

    for h in range(MQW // HD):
        hs = slice(h * HD, (h + 1) * HD)
        s = _dot_nt(mq_ref[:, hs], mk_ref[:, hs]) * scale
        m = jnp.max(s, axis=-1, keepdims=True)
        p = jnp.exp(s - m)
        probs = (p / jnp.sum(p, axis=-1, keepdims=True)).astype(bf16)
        mo_ref[:, hs] = _dot(probs, mv_ref[:, hs]).astype(bf16)


def _attn(sink, q, k, v, mq, mk, mv, bsz, seq, n_mem):
    nblk = seq // WINDOW
    n = bsz * seq
    cur = lambda b, j: (b * nblk + j, 0)
    prev = lambda b, j: (b * nblk + jnp.maximum(j - 1, 0), 0)
    nxt = lambda b, j: (b * nblk + jnp.minimum(j + 1, nblk - 1), 0)
    memb = lambda b, j: (b, 0)
    return pl.pallas_call(
        functools.partial(_attn_kernel, seq=seq),
        grid=(bsz, nblk),
        in_specs=[
            pl.BlockSpec(memory_space=pltpu.SMEM),
            pl.BlockSpec((WINDOW, QW), cur),
            pl.BlockSpec((WINDOW, KVW), prev),
            pl.BlockSpec((WINDOW, KVW), cur),
            pl.BlockSpec((WINDOW, KVW), nxt),
            pl.BlockSpec((WINDOW, KVW), prev),
            pl.BlockSpec((WINDOW, KVW), cur),
            pl.BlockSpec((WINDOW, KVW), nxt),
            pl.BlockSpec((WINDOW, MQW), cur),
            pl.BlockSpec((n_mem, MQW), memb),
            pl.BlockSpec((n_mem, MQW), memb),
        ],
        out_specs=[pl.BlockSpec((WINDOW, QW), cur), pl.BlockSpec((WINDOW, MQW), cur)],
        out_shape=[jax.ShapeDtypeStruct((n, QW), bf16), jax.ShapeDtypeStruct((n, MQW), bf16)],
        compiler_params=_cparams(("arbitrary", "arbitrary")),
        name="attn",
    )(sink, q, k, k, k, v, v, v, mq, mk, mv)


def _merge_kernel(x_ref, g1_ref, a_ref, ao_ref, mo_ref, wpp_ref, wap_ref, wmp_ref,
                  wg0_ref, wg1_ref, wg2_ref, bg0_ref, bg1_ref, bg2_ref, wout_ref,
                  g2_ref, wrh_ref, wrl_ref, br_ref,
                  h_ref, x32_ref, lg_ref, xn_ref):
    j = pl.program_id(1)

    @pl.when(j == 0)
    def _():
        x = x_ref[...]
        xn_ref[...] = _rms(x, g1_ref[...]).astype(bf16)
        h_ref[...] = x

    xn = xn_ref[...]
    gate0 = jax.nn.sigmoid(_dot(xn, wg0_ref[...]) + bg0_ref[...])
    gate1 = jax.nn.sigmoid(_dot(xn, wg1_ref[...]) + bg1_ref[...])
    gate2 = jax.nn.sigmoid(_dot(xn, wg2_ref[...]) + bg2_ref[...])
    merged = (gate0 * _dot(a_ref[...], wpp_ref[...])
              + gate1 * _dot(ao_ref[...], wap_ref[...])
              + gate2 * _dot(mo_ref[...], wmp_ref[...]))
    h_ref[...] += _dot(merged.astype(bf16), wout_ref[...])

    @pl.when(j == pl.num_programs(1) - 1)
    def _():
        xn2 = _rms(h_ref[...], g2_ref[...])
        hi = xn2.astype(bf16)
        lo = (xn2 - hi.astype(f32)).astype(bf16)
        lg_ref[...] = (_dot_nt(wrh_ref[...], hi) + _dot_nt(wrl_ref[...], hi)
                       + _dot_nt(wrh_ref[...], lo) + br_ref[...])
        _pack_store(x32_ref, hi, TM_MG)


def _merge(x2, g1, a, ao, mo, wpp, wap, wmp, w_in_bf, b_gate, wout, g2, wr_hi, wr_lo, br_col):
    n = x2.shape[0]
    nt = D // TN_MG
    goff = A_WIDTH // TN_MG
    row = lambda i, j: (i, 0)
    col = lambda i, j: (0, j)
    const = lambda i, j: (0, 0)
    return pl.pallas_call(
        _merge_kernel,
        grid=(n // TM_MG, nt),
        in_specs=[
            pl.BlockSpec((TM_MG, D), row),
            pl.BlockSpec((1, D), const),
            pl.BlockSpec((TM_MG, POOL_W), row),
            pl.BlockSpec((TM_MG, QW), row),
            pl.BlockSpec((TM_MG, MQW), row),
            pl.BlockSpec((POOL_W, TN_MG), col),
            pl.BlockSpec((QW, TN_MG), col),
            pl.BlockSpec((MQW, TN_MG), col),
            pl.BlockSpec((D, TN_MG), lambda i, j: (0, goff + j)),
            pl.BlockSpec((D, TN_MG), lambda i, j: (0, goff + nt + j)),
            pl.BlockSpec((D, TN_MG), lambda i, j: (0, goff + 2 * nt + j)),
            pl.BlockSpec((1, TN_MG), lambda i, j: (0, j)),
            pl.BlockSpec((1, TN_MG), lambda i, j: (0, nt + j)),
            pl.BlockSpec((1, TN_MG), lambda i, j: (0, 2 * nt + j)),
            pl.BlockSpec((TN_MG, D), lambda i, j: (j, 0)),
            pl.BlockSpec((1, D), const),
            pl.BlockSpec((N_EXP, D), const),
            pl.BlockSpec((N_EXP, D), const),
            pl.BlockSpec((N_EXP, 1), const),
        ],
        out_specs=[
            pl.BlockSpec((TM_MG, D), row),
            pl.BlockSpec((TM_MG * SLAB, LANES), row),
            pl.BlockSpec((N_EXP, TM_MG), lambda i, j: (0, i)),
        ],
        out_shape=[
            jax.ShapeDtypeStruct((n, D), f32),
            jax.ShapeDtypeStruct((n * SLAB, LANES), u32),
            jax.ShapeDtypeStruct((N_EXP, n), f32),
        ],
        scratch_shapes=[pltpu.VMEM((TM_MG, D), bf16)],
        compiler_params=_cparams(("arbitrary", "arbitrary")),
        name="merge",
    )(x2, g1, a, ao, mo, wpp, wap, wmp, w_in_bf, w_in_bf, w_in_bf, b_gate, b_gate, b_gate, wout,
      g2, wr_hi, wr_lo, br_col)


def _route_kernel(lg_ref, tri_ref, dest_ref, w_ref, cnt_ref, cnt_scr, carry_scr):
    p = pl.program_id(0)
    i = pl.program_id(1)

    @pl.when((p == 0) & (i == 0))
    def _():
        cnt_scr[...] = jnp.zeros_like(cnt_scr)

    @pl.when(i == 0)
    def _():
        carry_scr[...] = jnp.zeros_like(carry_scr)

    l = lg_ref[...]
    eidx = lax.broadcasted_iota(i32, l.shape, 0)
    sels, tops = [], []
    for _ in range(TOP_K):
        m = jnp.max(l, axis=0, keepdims=True)
        idx = jnp.min(jnp.where(l == m, eidx, N_EXP), axis=0, keepdims=True)
        sel = eidx == idx
        l = jnp.where(sel, -jnp.inf, l)
        sels.append(sel)
        tops.append(m)
    es = [jnp.exp(t - tops[0]) for t in tops]
    tot = es[0] + es[1] + es[2] + es[3]
    w_ref[...] = jnp.concatenate([e / tot for e in es], axis=0)

    onehot = jnp.zeros(l.shape, f32)
    for sel in sels:
        onehot = jnp.where(sel, 1.0, onehot)
    blk_cnt = jnp.sum(onehot, axis=1, keepdims=True)

    @pl.when(p == 0)
    def _():
        cnt_scr[...] += blk_cnt

    cnt = cnt_scr[...][:, 0:1].astype(i32)
    padded = ((cnt + (MOE_BLOCK - 1)) // MOE_BLOCK * MOE_BLOCK).astype(f32)
    rr = lax.broadcasted_iota(i32, (N_EXP, N_EXP), 0)
    cc = lax.broadcasted_iota(i32, (N_EXP, N_EXP), 1)
    start_row = jnp.sum(jnp.where(rr < cc, padded, 0.0), axis=0, keepdims=True)
    start_col = jnp.sum(jnp.where(rr == cc, start_row, 0.0), axis=1, keepdims=True)
    cum = _dot(onehot.astype(bf16), tri_ref[...])
    val = cum + start_col + carry_scr[...][:, 0:1]
    dest_ref[...] = jnp.concatenate(
        [jnp.sum(jnp.where(sel, val, 0.0), axis=0, keepdims=True) for sel in sels], axis=0).astype(i32)
    carry_scr[...] += blk_cnt
    cnt_ref[...] = cnt_scr[...]


def _route(logits_t, tri):
    n = logits_t.shape[1]
    blk = lambda p, i: (0, i)
    final = lambda p, i: (0, i * p)
    return pl.pallas_call(
        _route_kernel,
        grid=(2, n // TR),
        in_specs=[pl.BlockSpec((N_EXP, TR), blk), pl.BlockSpec((TR, TR), lambda p, i: (0, 0))],
        out_specs=[
            pl.BlockSpec((TOP_K, TR), final),
            pl.BlockSpec((TOP_K, TR), final),
            pl.BlockSpec((N_EXP, LANES), lambda p, i: (0, 0)),
        ],
        out_shape=[
            jax.ShapeDtypeStruct((TOP_K, n), i32),
            jax.ShapeDtypeStruct((TOP_K, n), f32),
            jax.ShapeDtypeStruct((N_EXP, LANES), f32),
        ],
        scratch_shapes=[pltpu.VMEM((N_EXP, LANES), f32), pltpu.VMEM((N_EXP, LANES), f32)],
        compiler_params=_cparams(("arbitrary", "arbitrary")),
        name="route",
    )(logits_t, tri)


def _dispatch_kernel(dest_ref, x_ref, init_hbm, xs_hbm, sem, *, n_tok):
    del init_hbm
    base = pl.program_id(0) * TD

    def body(t, carry):
        src = x_ref.at[pl.ds(pl.multiple_of(t * SLAB, SLAB), SLAB)]
        for k in range(TOP_K):
            d = dest_ref[k * n_tok + base + t]
            pltpu.make_async_copy(src, xs_hbm.at[pl.ds(pl.multiple_of(d * SLAB, SLAB), SLAB)], sem).start()
        return carry

    lax.fori_loop(0, TD, body, 0)
    rows = TOP_K * TD * SLAB
    pltpu.make_async_copy(xs_hbm.at[pl.ds(0, rows)], xs_hbm.at[pl.ds(0, rows)], sem).wait()


def _dispatch(dest_flat, x32, xs_init, n_tok):
    return pl.pallas_call(
        functools.partial(_dispatch_kernel, n_tok=n_tok),
        grid_spec=pltpu.PrefetchScalarGridSpec(
            num_scalar_prefetch=1,
            grid=(n_tok // TD,),
            in_specs=[pl.BlockSpec((TD * SLAB, LANES), lambda i, d: (i, 0)), pl.BlockSpec(memory_space=pl.ANY)],
            out_specs=pl.BlockSpec(memory_space=pl.ANY),
            scratch_shapes=[pltpu.SemaphoreType.DMA(())],
        ),
        out_shape=jax.ShapeDtypeStruct(xs_init.shape, u32),
        input_output_aliases={2: 0},
        compiler_params=_cparams(("arbitrary",)),
        name="dispatch",
    )(dest_flat, x32, xs_init)


CAST_ROWS = 256


def _cast_weights(w_ref, wbf_ref):
    def body(r, carry):
        r0 = pl.multiple_of(r * CAST_ROWS, CAST_ROWS)
        wbf_ref[pl.ds(r0, CAST_ROWS), :] = w_ref[pl.ds(r0, CAST_ROWS), :].astype(bf16)
        return carry

    lax.fori_loop(0, w_ref.shape[0] // CAST_ROWS, body, 0)


def _expert_changed(be_ref, i):
    return (i == 0) | (be_ref[i] != be_ref[jnp.maximum(i - 1, 0)])


def _moe_up_kernel(be_ref, nr_ref, x_ref, w_ref, b_ref, sel_ref, act_ref, wbf_ref):
    i = pl.program_id(1)

    @pl.when(_expert_changed(be_ref, i))
    def _():
        _cast_weights(w_ref, wbf_ref)

    def compute(rows):
        lo_hi = [_unpack_chunk(x_ref[pl.ds(c, rows, stride=SLAB), :]) for c in range(SLAB)]
        x = jnp.concatenate([p[0].astype(bf16) for p in lo_hi] + [p[1].astype(bf16) for p in lo_hi], axis=1)
        n_chunks = TN_UP // UP_CHUNK

        def pre_act(n):
            ns = slice(n * UP_CHUNK, (n + 1) * UP_CHUNK)
            return _dot(x, wbf_ref[:, ns]) + b_ref[:, ns]

        hb_next = pre_act(0)
        for n in range(n_chunks):
            hb = hb_next
            if n + 1 < n_chunks:
                hb_next = pre_act(n + 1)
            gate = jnp.minimum(hb, LIMIT)
            up = jnp.clip(hb, -LIMIT, LIMIT)
            up_at_even = pltpu.roll(up, UP_CHUNK - 1, 1)
            act = (gate * jax.nn.sigmoid(ALPHA * gate) * (up_at_even + 1.0)).astype(bf16)
            for c in range(UP_CHUNK // (2 * LANES)):
                o0 = n * (UP_CHUNK // 2) + c * LANES
                act_ref[0:rows, o0:o0 + LANES] = _dot(
                    act[:, c * 2 * LANES:(c + 1) * 2 * LANES], sel_ref[...]).astype(bf16)
        if rows < MOE_BLOCK:
            act_ref[rows:, :] = jnp.zeros((MOE_BLOCK - rows, TN_UP // 2), bf16)

    n_rows = nr_ref[i]

    @pl.when(n_rows > HALF_BLOCK)
    def _():
        compute(MOE_BLOCK)

    @pl.when((n_rows > 0) & (n_rows <= HALF_BLOCK))
    def _():
        compute(HALF_BLOCK)

    @pl.when(n_rows == 0)
    def _():
        act_ref[...] = jnp.zeros_like(act_ref)


def _moe_up(blk_expert, blk_rows, xs, w_up, b_up3, sel, cap):
    n_blocks = cap // MOE_BLOCK
    return pl.pallas_call(
        _moe_up_kernel,
        grid_spec=pltpu.PrefetchScalarGridSpec(
            num_scalar_prefetch=2,
            grid=(2 * D_FF // TN_UP, n_blocks),
            in_specs=[
                pl.BlockSpec((MOE_BLOCK * SLAB, LANES), lambda j, i, be, nb: (i, 0)),
                pl.BlockSpec((None, D, TN_UP), lambda j, i, be, nb: (be[i], 0, j)),
                pl.BlockSpec((None, 1, TN_UP), lambda j, i, be, nb: (be[i], 0, j)),
                pl.BlockSpec((2 * LANES, LANES), lambda j, i, be, nb: (0, 0)),
            ],
            out_specs=pl.BlockSpec((MOE_BLOCK, TN_UP // 2), lambda j, i, be, nb: (i, j)),
            scratch_shapes=[pltpu.VMEM((D, TN_UP), bf16)],
        ),
        out_shape=jax.ShapeDtypeStruct((cap, D_FF), bf16),
        compiler_params=_cparams(("arbitrary", "arbitrary")),
        name="moe_up",
    )(blk_expert, blk_rows, xs, w_up, b_up3, sel)


def _moe_down_kernel(be_ref, nr_ref, a_ref, w_ref, b_ref, y_ref, wbf_ref):
    i = pl.program_id(0)

    @pl.when(_expert_changed(be_ref, i))
    def _():
        _cast_weights(w_ref, wbf_ref)

    def compute(rows):
        y = _dot(a_ref[0:rows, :], wbf_ref[...]) + b_ref[...]
        _pack_store(y_ref, y.astype(bf16), rows)
        if rows < MOE_BLOCK:
            y_ref[rows * SLAB:, :] = jnp.zeros(((MOE_BLOCK - rows) * SLAB, LANES), u32)

    n_rows = nr_ref[i]

    @pl.when(n_rows > HALF_BLOCK)
    def _():
        compute(MOE_BLOCK)

    @pl.when((n_rows > 0) & (n_rows <= HALF_BLOCK))
    def _():
        compute(HALF_BLOCK)

    @pl.when(n_rows == 0)
    def _():
        y_ref[...] = jnp.zeros_like(y_ref)


def _moe_down(blk_expert, blk_rows, act, w_down, b_down3, cap):
    n_blocks = cap // MOE_BLOCK
    return pl.pallas_call(
        _moe_down_kernel,
        grid_spec=pltpu.PrefetchScalarGridSpec(
            num_scalar_prefetch=2,
            grid=(n_blocks,),
            in_specs=[
                pl.BlockSpec((MOE_BLOCK, D_FF), lambda i, be, nb: (i, 0)),
                pl.BlockSpec((None, D_FF, D), lambda i, be, nb: (be[i], 0, 0)),
                pl.BlockSpec((None, 1, D), lambda i, be, nb: (be[i], 0, 0)),
            ],
            out_specs=pl.BlockSpec((MOE_BLOCK * SLAB, LANES), lambda i, be, nb: (i, 0)),
            scratch_shapes=[pltpu.VMEM((D_FF, D), bf16)],
        ),
        out_shape=jax.ShapeDtypeStruct((cap * SLAB, LANES), u32),
        compiler_params=_cparams(("arbitrary",)),
        name="moe_down",
    )(blk_expert, blk_rows, act, w_down, b_down3)


def _combine_kernel(dest_ref, y_hbm, h_ref, w_ref, o_ref, buf0, buf1, sem0, sem1, *, n_tok):
    i = pl.program_id(0)

    def issue(tb, buf, sem):
        base = tb * TC

        def body(t, carry):
            for k in range(TOP_K):
                d = dest_ref[k * n_tok + base + t]
                pltpu.make_async_copy(
                    y_hbm.at[pl.ds(pl.multiple_of(d * SLAB, SLAB), SLAB)],
                    buf.at[pl.ds(pl.multiple_of((k * TC + t) * SLAB, SLAB), SLAB)], sem).start()
            return carry

        lax.fori_loop(0, TC, body, 0)

    def wait(buf, sem):
        pltpu.make_async_copy(y_hbm.at[pl.ds(0, TOP_K * TC * SLAB)], buf, sem).wait()

    def compute(half, buf):
        rows = slice(half * TC, (half + 1) * TC)
        wv = w_ref[rows, :]
        for c in range(SLAB):
            cl = slice(c * LANES, (c + 1) * LANES)
            ch = slice(HALF + c * LANES, HALF + (c + 1) * LANES)
            acc_lo = h_ref[rows, cl]
            acc_hi = h_ref[rows, ch]
            for k in range(TOP_K):
                lo, hi = _unpack_chunk(buf[pl.ds(k * TC * SLAB + c, TC, stride=SLAB), :])
                acc_lo = acc_lo + wv[:, k:k + 1] * lo
                acc_hi = acc_hi + wv[:, k:k + 1] * hi
            o_ref[rows, cl] = acc_lo
            o_ref[rows, ch] = acc_hi

    @pl.when(i == 0)
    def _():
        issue(0, buf0, sem0)

    issue(2 * i + 1, buf1, sem1)
    wait(buf0, sem0)
    compute(0, buf0)

    @pl.when(i + 1 < pl.num_programs(0))
    def _():
        issue(2 * i + 2, buf0, sem0)

    wait(buf1, sem1)
    compute(1, buf1)


def _combine(dest_flat, y32, h, w_tok, n_tok):
    return pl.pallas_call(
        functools.partial(_combine_kernel, n_tok=n_tok),
        grid_spec=pltpu.PrefetchScalarGridSpec(
            num_scalar_prefetch=1,
            grid=(n_tok // (2 * TC),),
            in_specs=[
                pl.BlockSpec(memory_space=pl.ANY),
                pl.BlockSpec((2 * TC, D), lambda i, d: (i, 0)),
                pl.BlockSpec((2 * TC, TOP_K), lambda i, d: (i, 0)),
            ],
            out_specs=pl.BlockSpec((2 * TC, D), lambda i, d: (i, 0)),
            scratch_shapes=[
                pltpu.VMEM((TOP_K * TC * SLAB, LANES), u32),
                pltpu.VMEM((TOP_K * TC * SLAB, LANES), u32),
                pltpu.SemaphoreType.DMA(()),
                pltpu.SemaphoreType.DMA(()),
            ],
        ),
        out_shape=jax.ShapeDtypeStruct((n_tok, D), f32),
        compiler_params=_cparams(("arbitrary",)),
        name="combine",
    )(dest_flat, y32, h, w_tok)


def kernel(x, mem, positions, norm1_gain, w_in, b_gate, w_pool_group, pool_scale, w_pool_proj, attn_q_norm, attn_k_norm, attn_sink, w_attn_proj, mem_norm_gain, w_mem_kv, mem_q_norm, mem_k_norm, w_mem_proj, w_out, norm2_gain, w_router, b_router, w_up, b_up, w_down, b_down):
    bsz, seq, d = x.shape
    n_mem = mem.shape[1]
    n_tok = bsz * seq
    n_assign = n_tok * TOP_K
    n_blocks = -(-n_assign // MOE_BLOCK) + N_EXP
    cap = n_blocks * MOE_BLOCK
    depth = norm1_gain.shape[0]

    half = HD // 2
    inv_freq = jnp.power(jnp.float32(THETA), -jnp.arange(half, dtype=f32) * (2.0 / HD))
    invf = jnp.concatenate([inv_freq, inv_freq])[None, :]
    pos_col = positions.reshape(n_tok, 1)
    tri = jnp.asarray(np.triu(np.ones((TR, TR), np.float32), 1), bf16)
    sel_np = np.zeros((2 * LANES, LANES), np.float32)
    sel_np[2 * np.arange(LANES), np.arange(LANES)] = 1.0
    sel = jnp.asarray(sel_np, bf16)

    h2 = x.reshape(n_tok, d)
    mem2 = mem.reshape(bsz * n_mem, d)
    for l in range(depth):
        row = lambda v: v[l][None, :]
        w_in_bf = w_in[l].astype(bf16)
        pool_u, q, k, v, mq = _inproj(h2, row(norm1_gain), w_in_bf, pos_col, invf,
                                      row(attn_q_norm), row(attn_k_norm), row(mem_q_norm))
        a = _pool(pool_u.reshape(bsz, seq, POOL_W), w_pool_group[l].astype(bf16), row(pool_scale))
        mk, mv = _memkv(mem2, row(mem_norm_gain), w_mem_kv[l].astype(bf16), row(mem_k_norm), n_mem)
        ao, mo = _attn(attn_sink[l], q, k, v, mq, mk, mv, bsz, seq, n_mem)
        wr_t = w_router[l].T
        wr_hi = wr_t.astype(bf16)
        wr_lo = (wr_t - wr_hi.astype(f32)).astype(bf16)
        hmid, x32, logits_t = _merge(
            h2, row(norm1_gain), a.reshape(n_tok, POOL_W), ao, mo,
            w_pool_proj[l].astype(bf16), w_attn_proj[l].astype(bf16), w_mem_proj[l].astype(bf16),
            w_in_bf, row(b_gate), w_out[l].astype(bf16), row(norm2_gain), wr_hi, wr_lo,
            b_router[l][:, None])

        dest, w_top, cnt = _route(logits_t, tri)
        counts = cnt[:, 0].astype(i32)
        padded = (counts + MOE_BLOCK - 1) // MOE_BLOCK * MOE_BLOCK
        pad_ends = jnp.cumsum(padded)
        blk_start = jnp.arange(n_blocks, dtype=i32) * MOE_BLOCK
        blk_expert = jnp.minimum(jnp.sum((pad_ends[None, :] <= blk_start[:, None]).astype(i32), axis=1),
                                 N_EXP - 1)
        first_blk = (pad_ends - padded) // MOE_BLOCK
        blk_idx = jnp.arange(n_blocks, dtype=i32)
        blk_rows = jnp.clip(counts[blk_expert] - (blk_idx - first_blk[blk_expert]) * MOE_BLOCK,
                           0, MOE_BLOCK).astype(i32)
        dest_flat = dest.reshape(-1)

        xs32 = _dispatch(dest_flat, x32, jnp.zeros((cap * SLAB, LANES), u32), n_tok)
        act = _moe_up(blk_expert, blk_rows, xs32, w_up[l], b_up[l][:, None, :], sel, cap)
        y32 = _moe_down(blk_expert, blk_rows, act, w_down[l], b_down[l][:, None, :], cap)
        h2 = _combine(dest_flat, y32, hmid, w_top.T, n_tok)
    return h2.reshape(bsz, seq, d)
```

```python
import functools

import numpy as np
import jax
import jax.numpy as jnp
from jax import lax
from jax.experimental import pallas as pl
from jax.experimental.pallas import tpu as pltpu

f32 = jnp.float32
bf16 = jnp.bfloat16
i32 = jnp.int32
u32 = jnp.uint32

D = 2048
HD = 128
POOL_WINDOWS = (2, 4, 8, 16)
POOL_W = 512
QW = 1024
KVW = 256
MQW = 512
A_WIDTH = POOL_W + QW + 2 * KVW + MQW
N_EXP = 32
TOP_K = 4
D_FF = 2048
LIMIT = 7.0
ALPHA = 1.702
MOE_BLOCK = 512
HALF_BLOCK = MOE_BLOCK // 2
EPS = 1e-6
NEG = -1e30
THETA = 10000.0
WINDOW = 128

LANES = 128
SLAB = 8
HALF = D // 2
VMEM_LIMIT = 56 * 1024 * 1024

TM_IN = 512
TM_MG = 512
TN_MG = 256
TR = 512
TD = 512
TC = 128
TN_UP = 2048
UP_CHUNK = 512
POOL_CHUNK = 512
HALO = 8


def _cparams(sem):
    return pltpu.CompilerParams(dimension_semantics=sem, vmem_limit_bytes=VMEM_LIMIT)


def _dot(a, b):
    return jnp.dot(a, b, preferred_element_type=f32)


def _dot_nt(a, b):
    return lax.dot_general(a, b, (((1,), (1,)), ((), ())), preferred_element_type=f32)


def _rms(x, gain):
    return x * lax.rsqrt(jnp.mean(x * x, axis=-1, keepdims=True) + EPS) * gain


def _bits(x_f32):
    return lax.bitcast_convert_type(x_f32, u32)


def _unbits(x_u32):
    return lax.bitcast_convert_type(x_u32, f32)


def _pack_store(dst_ref, vals_bf16, rows):
    bits = _bits(vals_bf16.astype(f32))
    for c in range(SLAB):
        lo = bits[:, c * LANES:(c + 1) * LANES] >> 16
        hi = bits[:, HALF + c * LANES:HALF + (c + 1) * LANES] & jnp.uint32(0xFFFF0000)
        dst_ref[pl.ds(c, rows, stride=SLAB), :] = hi | lo


def _unpack_chunk(words):
    lo = _unbits(words << 16)
    hi = _unbits(words & jnp.uint32(0xFFFF0000))
    return lo, hi


def _inproj_kernel(x_ref, g1_ref, w_ref, pos_ref, invf_ref, qg_ref, kg_ref, mg_ref,
                   pool_ref, q_ref, k_ref, v_ref, mq_ref):
    x = x_ref[...]
    xn = _rms(x, g1_ref[...]).astype(bf16)
    ang = pos_ref[...].astype(f32) * invf_ref[...]
    cos = jnp.cos(ang)
    sin = jnp.sin(ang)
    lane = lax.broadcasted_iota(i32, ang.shape, 1)
    sin_s = jnp.where(lane < HD // 2, -sin, sin)

    def rope(u):
        return u * cos + pltpu.roll(u, HD // 2, 1) * sin_s

    pool_ref[...] = _dot(xn, w_ref[:, 0:POOL_W])
    o0 = POOL_W
    uq = _dot(xn, w_ref[:, o0:o0 + QW])
    for h in range(QW // HD):
        q_ref[:, h * HD:(h + 1) * HD] = rope(_rms(uq[:, h * HD:(h + 1) * HD], qg_ref[...])).astype(bf16)
    o0 += QW
    uk = _dot(xn, w_ref[:, o0:o0 + KVW])
    for h in range(KVW // HD):
        k_ref[:, h * HD:(h + 1) * HD] = rope(_rms(uk[:, h * HD:(h + 1) * HD], kg_ref[...])).astype(bf16)
    o0 += KVW
    v_ref[...] = _dot(xn, w_ref[:, o0:o0 + KVW]).astype(bf16)
    o0 += KVW
    um = _dot(xn, w_ref[:, o0:o0 + MQW])
    for h in range(MQW // HD):
        mq_ref[:, h * HD:(h + 1) * HD] = _rms(um[:, h * HD:(h + 1) * HD], mg_ref[...]).astype(bf16)


def _inproj(x2, g1, w_in_bf, pos_col, invf, qg, kg, mg):
    n = x2.shape[0]
    row = lambda i: (i, 0)
    const = lambda i: (0, 0)
    return pl.pallas_call(
        _inproj_kernel,
        grid=(n // TM_IN,),
        in_specs=[
            pl.BlockSpec((TM_IN, D), row),
            pl.BlockSpec((1, D), const),
            pl.BlockSpec((D, A_WIDTH), const),
            pl.BlockSpec((TM_IN, 1), row),
            pl.BlockSpec((1, HD), const),
            pl.BlockSpec((1, HD), const),
            pl.BlockSpec((1, HD), const),
            pl.BlockSpec((1, HD), const),
        ],
        out_specs=[
            pl.BlockSpec((TM_IN, POOL_W), row),
            pl.BlockSpec((TM_IN, QW), row),
            pl.BlockSpec((TM_IN, KVW), row),
            pl.BlockSpec((TM_IN, KVW), row),
            pl.BlockSpec((TM_IN, MQW), row),
        ],
        out_shape=[
            jax.ShapeDtypeStruct((n, POOL_W), f32),
            jax.ShapeDtypeStruct((n, QW), bf16),
            jax.ShapeDtypeStruct((n, KVW), bf16),
            jax.ShapeDtypeStruct((n, KVW), bf16),
            jax.ShapeDtypeStruct((n, MQW), bf16),
        ],
        compiler_params=_cparams(("arbitrary",)),
        name="inproj",
    )(x2, g1, w_in_bf, pos_col, invf, qg, kg, mg)


def _pool_kernel(u_ref, wg_ref, sc_ref, a_ref, pad_ref, *, seq):
    zeros = jnp.zeros((HALO, POOL_W), f32)
    pad_ref[0:HALO, :] = zeros
    pad_ref[seq + HALO:seq + 2 * HALO, :] = zeros
    pad_ref[HALO:seq + HALO, :] = u_ref[0]

    def chunk(ci, carry):
        c0 = pl.multiple_of(ci * POOL_CHUNK, POOL_CHUNK)
        pos = c0 + lax.broadcasted_iota(i32, (POOL_CHUNK, 1), 0)
        for g, w in enumerate(POOL_WINDOWS):
            ext = pad_ref[pl.ds(c0, POOL_CHUNK + 2 * HALO), g * HD:(g + 1) * HD]
            tok = ext[HALO:HALO + POOL_CHUNK]
            win = None
            for o in range(-(w // 2), w // 2):
                term = ext[HALO + o:HALO + o + POOL_CHUNK]
                win = term if win is None else win + term
            lo = jnp.clip(pos - w // 2, 0, seq)
            hi = jnp.clip(pos - w // 2 + w, 0, seq)
            cnt = (hi - lo).astype(f32)
            pooled = (win / cnt - tok).astype(bf16)
            a = _dot(pooled, wg_ref[g]) * sc_ref[:, g * HD:(g + 1) * HD]
            a_ref[0, pl.ds(c0, POOL_CHUNK), g * HD:(g + 1) * HD] = a.astype(bf16)
        return carry

    lax.fori_loop(0, seq // POOL_CHUNK, chunk, 0)


def _pool(u3, wg_bf, scale):
    b, seq, _ = u3.shape
    return pl.pallas_call(
        functools.partial(_pool_kernel, seq=seq),
        grid=(b,),
        in_specs=[
            pl.BlockSpec((1, seq, POOL_W), lambda i: (i, 0, 0)),
            pl.BlockSpec((len(POOL_WINDOWS), HD, HD), lambda i: (0, 0, 0)),
            pl.BlockSpec((1, POOL_W), lambda i: (0, 0)),
        ],
        out_specs=pl.BlockSpec((1, seq, POOL_W), lambda i: (i, 0, 0)),
        out_shape=jax.ShapeDtypeStruct((b, seq, POOL_W), bf16),
        scratch_shapes=[pltpu.VMEM((seq + 2 * HALO, POOL_W), f32)],
        compiler_params=_cparams(("arbitrary",)),
        name="pool",
    )(u3, wg_bf, scale)


def _memkv_kernel(m_ref, g_ref, w_ref, kg_ref, mk_ref, mv_ref):
    mn = _rms(m_ref[...], g_ref[...]).astype(bf16)
    kv = _dot(mn, w_ref[...])
    for h in range(MQW // HD):
        mk_ref[:, h * HD:(h + 1) * HD] = _rms(kv[:, h * HD:(h + 1) * HD], kg_ref[...]).astype(bf16)
    mv_ref[...] = kv[:, MQW:].astype(bf16)


def _memkv(mem2, gain, w_bf, kg, n_mem):
    rows = mem2.shape[0]
    return pl.pallas_call(
        _memkv_kernel,
        grid=(rows // n_mem,),
        in_specs=[
            pl.BlockSpec((n_mem, D), lambda i: (i, 0)),
            pl.BlockSpec((1, D), lambda i: (0, 0)),
            pl.BlockSpec((D, 2 * MQW), lambda i: (0, 0)),
            pl.BlockSpec((1, HD), lambda i: (0, 0)),
        ],
        out_specs=[pl.BlockSpec((n_mem, MQW), lambda i: (i, 0))] * 2,
        out_shape=[jax.ShapeDtypeStruct((rows, MQW), bf16)] * 2,
        compiler_params=_cparams(("arbitrary",)),
        name="memkv",
    )(mem2, gain, w_bf, kg)


def _attn_kernel(sink_ref, q_ref, kp_ref, kc_ref, kn_ref, vp_ref, vc_ref, vn_ref, mq_ref, mk_ref, mv_ref,
                 ao_ref, mo_ref, *, seq):
    blk = WINDOW
    nb = pl.program_id(1)
    scale = 1.0 / np.sqrt(HD)
    group = (QW // HD) // (KVW // HD)
    rows = group * blk

    r = lax.broadcasted_iota(i32, (rows, 3 * blk), 0)
    c = lax.broadcasted_iota(i32, (rows, 3 * blk), 1)
    qi = (r % blk) + blk
    key_abs = (nb - 1) * blk + c
    mask = (jnp.abs(c - qi) <= WINDOW) & (key_abs >= 0) & (key_abs < seq)
    rcol = lax.broadcasted_iota(i32, (rows, 1), 0) // blk

    for kh in range(KVW // HD):
        ks = slice(kh * HD, (kh + 1) * HD)
        kb = jnp.concatenate([kp_ref[:, ks], kc_ref[:, ks], kn_ref[:, ks]], axis=0)
        vb = jnp.concatenate([vp_ref[:, ks], vc_ref[:, ks], vn_ref[:, ks]], axis=0)
        qh = jnp.concatenate(
            [q_ref[:, (kh * group + g) * HD:(kh * group + g + 1) * HD] for g in range(group)], axis=0)
        s = _dot_nt(qh, kb) * scale
        s = jnp.where(mask, s, NEG)
        sk = jnp.zeros((rows, 1), f32)
        for g in range(group):
            sk = jnp.where(rcol == g, sink_ref[kh * group + g], sk)
        m = jnp.maximum(jnp.max(s, axis=-1, keepdims=True), sk)
        p = jnp.exp(s - m)
        denom = jnp.sum(p, axis=-1, keepdims=True) + jnp.exp(sk - m)
        probs = (p / denom).astype(bf16)
        o = _dot(probs, vb)
        for g in range(group):
            h = kh * group + g
            ao_ref[:, h * HD:(h + 1) * HD] = o[g * blk:(g + 1) * blk].astype(bf16)

    for h in range(MQW // HD):
        hs = slice(h * HD, (h + 1) * HD)
        s = _dot_nt(mq_ref[:, hs], mk_ref[:, hs]) * scale
        m = jnp.max(s, axis=-1, keepdims=True)
        p = jnp.exp(s - m)
        probs = (p / jnp.sum(p, axis=-1, keepdims=True)).astype(bf16)
        mo_ref[:, hs] = _dot(probs, mv_ref[:, hs]).astype(bf16)


def _attn(sink, q, k, v, mq, mk, mv, bsz, seq, n_mem):
    nblk = seq // WINDOW
    n = bsz * seq
    cur = lambda b, j: (b * nblk + j, 0)
    prev = lambda b, j: (b * nblk + jnp.maximum(j - 1, 0), 0)
    nxt = lambda b, j: (b * nblk + jnp.minimum(j + 1, nblk - 1), 0)
    memb = lambda b, j: (b, 0)
    return pl.pallas_call(
        functools.partial(_attn_kernel, seq=seq),
        grid=(bsz, nblk),
        in_specs=[
            pl.BlockSpec(memory_space=pltpu.SMEM),
            pl.BlockSpec((WINDOW, QW), cur),
            pl.BlockSpec((WINDOW, KVW), prev),
            pl.BlockSpec((WINDOW, KVW), cur),
            pl.BlockSpec((WINDOW, KVW), nxt),
            pl.BlockSpec((WINDOW, KVW), prev),
            pl.BlockSpec((WINDOW, KVW), cur),
            pl.BlockSpec((WINDOW, KVW), nxt),
            pl.BlockSpec((WINDOW, MQW), cur),
            pl.BlockSpec((n_mem, MQW), memb),
            pl.BlockSpec((n_mem, MQW), memb),
        ],
        out_specs=[pl.BlockSpec((WINDOW, QW), cur), pl.BlockSpec((WINDOW, MQW), cur)],
        out_shape=[jax.ShapeDtypeStruct((n, QW), bf16), jax.ShapeDtypeStruct((n, MQW), bf16)],
        compiler_params=_cparams(("arbitrary", "arbitrary")),
        name="attn",
    )(sink, q, k, k, k, v, v, v, mq, mk, mv)


def _merge_kernel(x_ref, g1_ref, a_ref, ao_ref, mo_ref, wpp_ref, wap_ref, wmp_ref,
                  wg0_ref, wg1_ref, wg2_ref, bg0_ref, bg1_ref, bg2_ref, wout_ref,
                  g2_ref, wrh_ref, wrl_ref, br_ref,
                  h_ref, x32_ref, lg_ref, xn_ref):
    j = pl.program_id(1)

    @pl.when(j == 0)
    def _():
        x = x_ref[...]
        xn_ref[...] = _rms(x, g1_ref[...]).astype(bf16)
        h_ref[...] = x

    xn = xn_ref[...]
    gate0 = jax.nn.sigmoid(_dot(xn, wg0_ref[...]) + bg0_ref[...])
    gate1 = jax.nn.sigmoid(_dot(xn, wg1_ref[...]) + bg1_ref[...])
    gate2 = jax.nn.sigmoid(_dot(xn, wg2_ref[...]) + bg2_ref[...])
    merged = (gate0 * _dot(a_ref[...], wpp_ref[...])
              + gate1 * _dot(ao_ref[...], wap_ref[...])
              + gate2 * _dot(mo_ref[...], wmp_ref[...]))
    h_ref[...] += _dot(merged.astype(bf16), wout_ref[...])

    @pl.when(j == pl.num_programs(1) - 1)
    def _():
        xn2 = _rms(h_ref[...], g2_ref[...])
        hi = xn2.astype(bf16)
        lo = (xn2 - hi.astype(f32)).astype(bf16)
        lg_ref[...] = (_dot_nt(wrh_ref[...], hi) + _dot_nt(wrl_ref[...], hi)
                       + _dot_nt(wrh_ref[...], lo) + br_ref[...])
        _pack_store(x32_ref, hi, TM_MG)


def _merge(x2, g1, a, ao, mo, wpp, wap, wmp, w_in_bf, b_gate, wout, g2, wr_hi, wr_lo, br_col):
    n = x2.shape[0]
    nt = D // TN_MG
    goff = A_WIDTH // TN_MG
    row = lambda i, j: (i, 0)
    col = lambda i, j: (0, j)
    const = lambda i, j: (0, 0)
    return pl.pallas_call(
        _merge_kernel,
        grid=(n // TM_MG, nt),
        in_specs=[
            pl.BlockSpec((TM_MG, D), row),
            pl.BlockSpec((1, D), const),
            pl.BlockSpec((TM_MG, POOL_W), row),
            pl.BlockSpec((TM_MG, QW), row),
            pl.BlockSpec((TM_MG, MQW), row),
            pl.BlockSpec((POOL_W, TN_MG), col),
            pl.BlockSpec((QW, TN_MG), col),
            pl.BlockSpec((MQW, TN_MG), col),
            pl.BlockSpec((D, TN_MG), lambda i, j: (0, goff + j)),
            pl.BlockSpec((D, TN_MG), lambda i, j: (0, goff + nt + j)),
            pl.BlockSpec((D, TN_MG), lambda i, j: (0, goff + 2 * nt + j)),
            pl.BlockSpec((1, TN_MG), lambda i, j: (0, j)),
            pl.BlockSpec((1, TN_MG), lambda i, j: (0, nt + j)),
            pl.BlockSpec((1, TN_MG), lambda i, j: (0, 2 * nt + j)),
            pl.BlockSpec((TN_MG, D), lambda i, j: (j, 0)),
            pl.BlockSpec((1, D), const),
            pl.BlockSpec((N_EXP, D), const),
            pl.BlockSpec((N_EXP, D), const),
            pl.BlockSpec((N_EXP, 1), const),
        ],
        out_specs=[
            pl.BlockSpec((TM_MG, D), row),
            pl.BlockSpec((TM_MG * SLAB, LANES), row),
            pl.BlockSpec((N_EXP, TM_MG), lambda i, j: (0, i)),
        ],
        out_shape=[
            jax.ShapeDtypeStruct((n, D), f32),
            jax.ShapeDtypeStruct((n * SLAB, LANES), u32),
            jax.ShapeDtypeStruct((N_EXP, n), f32),
        ],
        scratch_shapes=[pltpu.VMEM((TM_MG, D), bf16)],
        compiler_params=_cparams(("arbitrary", "arbitrary")),
        name="merge",
    )(x2, g1, a, ao, mo, wpp, wap, wmp, w_in_bf, w_in_bf, w_in_bf, b_gate, b_gate, b_gate, wout,
      g2, wr_hi, wr_lo, br_col)


def _route_kernel(lg_ref, tri_ref, dest_ref, w_ref, cnt_ref, cnt_scr, carry_scr):
    p = pl.program_id(0)
    i = pl.program_id(1)

    @pl.when((p == 0) & (i == 0))
    def _():
        cnt_scr[...] = jnp.zeros_like(cnt_scr)

    @pl.when(i == 0)
    def _():
        carry_scr[...] = jnp.zeros_like(carry_scr)

    l = lg_ref[...]
    eidx = lax.broadcasted_iota(i32, l.shape, 0)
    sels, tops = [], []
    for _ in range(TOP_K):
        m = jnp.max(l, axis=0, keepdims=True)
        idx = jnp.min(jnp.where(l == m, eidx, N_EXP), axis=0, keepdims=True)
        sel = eidx == idx
        l = jnp.where(sel, -jnp.inf, l)
        sels.append(sel)
        tops.append(m)
    es = [jnp.exp(t - tops[0]) for t in tops]
    tot = es[0] + es[1] + es[2] + es[3]
    w_ref[...] = jnp.concatenate([e / tot for e in es], axis=0)

    onehot = jnp.zeros(l.shape, f32)
    for sel in sels:
        onehot = jnp.where(sel, 1.0, onehot)
    blk_cnt = jnp.sum(onehot, axis=1, keepdims=True)

    @pl.when(p == 0)
    def _():
        cnt_scr[...] += blk_cnt

    cnt = cnt_scr[...][:, 0:1].astype(i32)
    padded = ((cnt + (MOE_BLOCK - 1)) // MOE_BLOCK * MOE_BLOCK).astype(f32)
    rr = lax.broadcasted_iota(i32, (N_EXP, N_EXP), 0)
    cc = lax.broadcasted_iota(i32, (N_EXP, N_EXP), 1)
    start_row = jnp.sum(jnp.where(rr < cc, padded, 0.0), axis=0, keepdims=True)
    start_col = jnp.sum(jnp.where(rr == cc, start_row, 0.0), axis=1, keepdims=True)
    cum = _dot(onehot.astype(bf16), tri_ref[...])
    val = cum + start_col + carry_scr[...][:, 0:1]
    dest_ref[...] = jnp.concatenate(
        [jnp.sum(jnp.where(sel, val, 0.0), axis=0, keepdims=True) for sel in sels], axis=0).astype(i32)
    carry_scr[...] += blk_cnt
    cnt_ref[...] = cnt_scr[...]


def _route(logits_t, tri):
    n = logits_t.shape[1]
    blk = lambda p, i: (0, i)
    final = lambda p, i: (0, i * p)
    return pl.pallas_call(
        _route_kernel,
        grid=(2, n // TR),
        in_specs=[pl.BlockSpec((N_EXP, TR), blk), pl.BlockSpec((TR, TR), lambda p, i: (0, 0))],
        out_specs=[
            pl.BlockSpec((TOP_K, TR), final),
            pl.BlockSpec((TOP_K, TR), final),
            pl.BlockSpec((N_EXP, LANES), lambda p, i: (0, 0)),
        ],
        out_shape=[
            jax.ShapeDtypeStruct((TOP_K, n), i32),
            jax.ShapeDtypeStruct((TOP_K, n), f32),
            jax.ShapeDtypeStruct((N_EXP, LANES), f32),
        ],
        scratch_shapes=[pltpu.VMEM((N_EXP, LANES), f32), pltpu.VMEM((N_EXP, LANES), f32)],
        compiler_params=_cparams(("arbitrary", "arbitrary")),
        name="route",
    )(logits_t, tri)


def _dispatch_kernel(dest_ref, x_ref, init_hbm, xs_hbm, sem, *, n_tok):
    del init_hbm
    base = pl.program_id(0) * TD

    def body(t, carry):
        src = x_ref.at[pl.ds(pl.multiple_of(t * SLAB, SLAB), SLAB)]
        for k in range(TOP_K):
            d = dest_ref[k * n_tok + base + t]
            pltpu.make_async_copy(src, xs_hbm.at[pl.ds(pl.multiple_of(d * SLAB, SLAB), SLAB)], sem).start()
        return carry

    lax.fori_loop(0, TD, body, 0)
    rows = TOP_K * TD * SLAB
    pltpu.make_async_copy(xs_hbm.at[pl.ds(0, rows)], xs_hbm.at[pl.ds(0, rows)], sem).wait()


def _dispatch(dest_flat, x32, xs_init, n_tok):
    return pl.pallas_call(
        functools.partial(_dispatch_kernel, n_tok=n_tok),
        grid_spec=pltpu.PrefetchScalarGridSpec(
            num_scalar_prefetch=1,
            grid=(n_tok // TD,),
            in_specs=[pl.BlockSpec((TD * SLAB, LANES), lambda i, d: (i, 0)), pl.BlockSpec(memory_space=pl.ANY)],
            out_specs=pl.BlockSpec(memory_space=pl.ANY),
            scratch_shapes=[pltpu.SemaphoreType.DMA(())],
        ),
        out_shape=jax.ShapeDtypeStruct(xs_init.shape, u32),
        input_output_aliases={2: 0},
        compiler_params=_cparams(("arbitrary",)),
        name="dispatch",
    )(dest_flat, x32, xs_init)


CAST_ROWS = 256


def _cast_weights(w_ref, wbf_ref):
    def body(r, carry):
        r0 = pl.multiple_of(r * CAST_ROWS, CAST_ROWS)
        wbf_ref[pl.ds(r0, CAST_ROWS), :] = w_ref[pl.ds(r0, CAST_ROWS), :].astype(bf16)
        return carry

    lax.fori_loop(0, w_ref.shape[0] // CAST_ROWS, body, 0)


def _expert_changed(be_ref, i):
    return (i == 0) | (be_ref[i] != be_ref[jnp.maximum(i - 1, 0)])


def _stream_expert_weights(be_ref, nx_ref, w_hbm, stage_ref, wbf_ref, sem, j, i, n_pass, width):
    def fetch(e, jj):
        cols = pl.ds(pl.multiple_of(jj * width, width), width)
        return pltpu.make_async_copy(w_hbm.at[e, :, cols], stage_ref, sem)

    @pl.when(_expert_changed(be_ref, i))
    def _():
        @pl.when((j == 0) & (i == 0))
        def _():
            fetch(be_ref[0], 0).start()

        fetch(be_ref[i], j).wait()
        _cast_weights(stage_ref, wbf_ref)
        nxt = nx_ref[i]

        @pl.when(nxt >= 0)
        def _():
            fetch(nxt, j).start()

        @pl.when((nxt < 0) & (j + 1 < n_pass))
        def _():
            fetch(be_ref[0], j + 1).start()


def _moe_up_kernel(be_ref, nr_ref, nx_ref, x_ref, w_hbm, b_ref, sel_ref, act_ref, wbf_ref, stage_ref, sem):
    i = pl.program_id(1)
    _stream_expert_weights(be_ref, nx_ref, w_hbm, stage_ref, wbf_ref, sem,
                           pl.program_id(0), i, pl.num_programs(0), TN_UP)

    def compute(rows):
        lo_hi = [_unpack_chunk(x_ref[pl.ds(c, rows, stride=SLAB), :]) for c in range(SLAB)]
        x = jnp.concatenate([p[0].astype(bf16) for p in lo_hi] + [p[1].astype(bf16) for p in lo_hi], axis=1)
        n_chunks = TN_UP // UP_CHUNK

        def pre_act(n):
            ns = slice(n * UP_CHUNK, (n + 1) * UP_CHUNK)
            return _dot(x, wbf_ref[:, ns]) + b_ref[:, ns]

        hb_next = pre_act(0)
        for n in range(n_chunks):
            hb = hb_next
            if n + 1 < n_chunks:
                hb_next = pre_act(n + 1)
            gate = jnp.minimum(hb, LIMIT)
            up = jnp.clip(hb, -LIMIT, LIMIT)
            up_at_even = pltpu.roll(up, UP_CHUNK - 1, 1)
            act = (gate * jax.nn.sigmoid(ALPHA * gate) * (up_at_even + 1.0)).astype(bf16)
            for c in range(UP_CHUNK // (2 * LANES)):
                o0 = n * (UP_CHUNK // 2) + c * LANES
                act_ref[0:rows, o0:o0 + LANES] = _dot(
                    act[:, c * 2 * LANES:(c + 1) * 2 * LANES], sel_ref[...]).astype(bf16)
        if rows < MOE_BLOCK:
            act_ref[rows:, :] = jnp.zeros((MOE_BLOCK - rows, TN_UP // 2), bf16)

    n_rows = nr_ref[i]

    @pl.when(n_rows > HALF_BLOCK)
    def _():
        compute(MOE_BLOCK)

    @pl.when((n_rows > 0) & (n_rows <= HALF_BLOCK))
    def _():
        compute(HALF_BLOCK)

    @pl.when(n_rows == 0)
    def _():
        act_ref[...] = jnp.zeros_like(act_ref)


def _moe_up(blk_expert, blk_rows, next_expert, xs, w_up, b_up3, sel, cap):
    n_blocks = cap // MOE_BLOCK
    return pl.pallas_call(
        _moe_up_kernel,
        grid_spec=pltpu.PrefetchScalarGridSpec(
            num_scalar_prefetch=3,
            grid=(2 * D_FF // TN_UP, n_blocks),
            in_specs=[
                pl.BlockSpec((MOE_BLOCK * SLAB, LANES), lambda j, i, be, nr, nx: (i, 0)),
                pl.BlockSpec(memory_space=pl.ANY),
                pl.BlockSpec((None, 1, TN_UP), lambda j, i, be, nr, nx: (be[i], 0, j)),
                pl.BlockSpec((2 * LANES, LANES), lambda j, i, be, nr, nx: (0, 0)),
            ],
            out_specs=pl.BlockSpec((MOE_BLOCK, TN_UP // 2), lambda j, i, be, nr, nx: (i, j)),
            scratch_shapes=[pltpu.VMEM((D, TN_UP), bf16), pltpu.VMEM((D, TN_UP), f32),
                            pltpu.SemaphoreType.DMA(())],
        ),
        out_shape=jax.ShapeDtypeStruct((cap, D_FF), bf16),
        compiler_params=_cparams(("arbitrary", "arbitrary")),
        name="moe_up",
    )(blk_expert, blk_rows, next_expert, xs, w_up, b_up3, sel)


def _moe_down_kernel(be_ref, nr_ref, nx_ref, a_ref, w_hbm, b_ref, y_ref, wbf_ref, stage_ref, sem):
    i = pl.program_id(0)
    _stream_expert_weights(be_ref, nx_ref, w_hbm, stage_ref, wbf_ref, sem, 0, i, 1, D)

    def compute(rows):
        y = _dot(a_ref[0:rows, :], wbf_ref[...]) + b_ref[...]
        _pack_store(y_ref, y.astype(bf16), rows)
        if rows < MOE_BLOCK:
            y_ref[rows * SLAB:, :] = jnp.zeros(((MOE_BLOCK - rows) * SLAB, LANES), u32)

    n_rows = nr_ref[i]

    @pl.when(n_rows > HALF_BLOCK)
    def _():
        compute(MOE_BLOCK)

    @pl.when((n_rows > 0) & (n_rows <= HALF_BLOCK))
    def _():
        compute(HALF_BLOCK)

    @pl.when(n_rows == 0)
    def _():
        y_ref[...] = jnp.zeros_like(y_ref)


def _moe_down(blk_expert, blk_rows, next_expert, act, w_down, b_down3, cap):
    n_blocks = cap // MOE_BLOCK
    return pl.pallas_call(
        _moe_down_kernel,
        grid_spec=pltpu.PrefetchScalarGridSpec(
            num_scalar_prefetch=3,
            grid=(n_blocks,),
            in_specs=[
                pl.BlockSpec((MOE_BLOCK, D_FF), lambda i, be, nr, nx: (i, 0)),
                pl.BlockSpec(memory_space=pl.ANY),
                pl.BlockSpec((None, 1, D), lambda i, be, nr, nx: (be[i], 0, 0)),
            ],
            out_specs=pl.BlockSpec((MOE_BLOCK * SLAB, LANES), lambda i, be, nr, nx: (i, 0)),
            scratch_shapes=[pltpu.VMEM((D_FF, D), bf16), pltpu.VMEM((D_FF, D), f32),
                            pltpu.SemaphoreType.DMA(())],
        ),
        out_shape=jax.ShapeDtypeStruct((cap * SLAB, LANES), u32),
        compiler_params=_cparams(("arbitrary",)),
        name="moe_down",
    )(blk_expert, blk_rows, next_expert, act, w_down, b_down3)


def _combine_kernel(dest_ref, y_hbm, h_ref, w_ref, o_ref, buf0, buf1, sem0, sem1, *, n_tok):
    i = pl.program_id(0)

    def issue(tb, buf, sem):
        base = tb * TC

        def body(t, carry):
            for k in range(TOP_K):
                d = dest_ref[k * n_tok + base + t]
                pltpu.make_async_copy(
                    y_hbm.at[pl.ds(pl.multiple_of(d * SLAB, SLAB), SLAB)],
                    buf.at[pl.ds(pl.multiple_of((k * TC + t) * SLAB, SLAB), SLAB)], sem).start()
            return carry

        lax.fori_loop(0, TC, body, 0)

    def wait(buf, sem):
        pltpu.make_async_copy(y_hbm.at[pl.ds(0, TOP_K * TC * SLAB)], buf, sem).wait()

    def compute(half, buf):
        rows = slice(half * TC, (half + 1) * TC)
        wv = w_ref[rows, :]
        for c in range(SLAB):
            cl = slice(c * LANES, (c + 1) * LANES)
            ch = slice(HALF + c * LANES, HALF + (c + 1) * LANES)
            acc_lo = h_ref[rows, cl]
            acc_hi = h_ref[rows, ch]
            for k in range(TOP_K):
                lo, hi = _unpack_chunk(buf[pl.ds(k * TC * SLAB + c, TC, stride=SLAB), :])
                acc_lo = acc_lo + wv[:, k:k + 1] * lo
                acc_hi = acc_hi + wv[:, k:k + 1] * hi
            o_ref[rows, cl] = acc_lo
            o_ref[rows, ch] = acc_hi

    @pl.when(i == 0)
    def _():
        issue(0, buf0, sem0)

    issue(2 * i + 1, buf1, sem1)
    wait(buf0, sem0)
    compute(0, buf0)

    @pl.when(i + 1 < pl.num_programs(0))
    def _():
        issue(2 * i + 2, buf0, sem0)

    wait(buf1, sem1)
    compute(1, buf1)


def _combine(dest_flat, y32, h, w_tok, n_tok):
    return pl.pallas_call(
        functools.partial(_combine_kernel, n_tok=n_tok),
        grid_spec=pltpu.PrefetchScalarGridSpec(
            num_scalar_prefetch=1,
            grid=(n_tok // (2 * TC),),
            in_specs=[
                pl.BlockSpec(memory_space=pl.ANY),
                pl.BlockSpec((2 * TC, D), lambda i, d: (i, 0)),
                pl.BlockSpec((2 * TC, TOP_K), lambda i, d: (i, 0)),
            ],
            out_specs=pl.BlockSpec((2 * TC, D), lambda i, d: (i, 0)),
            scratch_shapes=[
                pltpu.VMEM((TOP_K * TC * SLAB, LANES), u32),
                pltpu.VMEM((TOP_K * TC * SLAB, LANES), u32),
                pltpu.SemaphoreType.DMA(()),
                pltpu.SemaphoreType.DMA(()),
            ],
        ),
        out_shape=jax.ShapeDtypeStruct((n_tok, D), f32),
        compiler_params=_cparams(("arbitrary",)),
        name="combine",
    )(dest_flat, y32, h, w_tok)


def kernel(x, mem, positions, norm1_gain, w_in, b_gate, w_pool_group, pool_scale, w_pool_proj, attn_q_norm, attn_k_norm, attn_sink, w_attn_proj, mem_norm_gain, w_mem_kv, mem_q_norm, mem_k_norm, w_mem_proj, w_out, norm2_gain, w_router, b_router, w_up, b_up, w_down, b_down):
    bsz, seq, d = x.shape
    n_mem = mem.shape[1]
    n_tok = bsz * seq
    n_assign = n_tok * TOP_K
    n_blocks = -(-n_assign // MOE_BLOCK) + N_EXP
    cap = n_blocks * MOE_BLOCK
    depth = norm1_gain.shape[0]

    half = HD // 2
    inv_freq = jnp.power(jnp.float32(THETA), -jnp.arange(half, dtype=f32) * (2.0 / HD))
    invf = jnp.concatenate([inv_freq, inv_freq])[None, :]
    pos_col = positions.reshape(n_tok, 1)
    tri = jnp.asarray(np.triu(np.ones((TR, TR), np.float32), 1), bf16)
    sel_np = np.zeros((2 * LANES, LANES), np.float32)
    sel_np[2 * np.arange(LANES), np.arange(LANES)] = 1.0
    sel = jnp.asarray(sel_np, bf16)

    h2 = x.reshape(n_tok, d)
    mem2 = mem.reshape(bsz * n_mem, d)
    for l in range(depth):
        row = lambda v: v[l][None, :]
        w_in_bf = w_in[l].astype(bf16)
        pool_u, q, k, v, mq = _inproj(h2, row(norm1_gain), w_in_bf, pos_col, invf,
                                      row(attn_q_norm), row(attn_k_norm), row(mem_q_norm))
        a = _pool(pool_u.reshape(bsz, seq, POOL_W), w_pool_group[l].astype(bf16), row(pool_scale))
        mk, mv = _memkv(mem2, row(mem_norm_gain), w_mem_kv[l].astype(bf16), row(mem_k_norm), n_mem)
        ao, mo = _attn(attn_sink[l], q, k, v, mq, mk, mv, bsz, seq, n_mem)
        wr_t = w_router[l].T
        wr_hi = wr_t.astype(bf16)
        wr_lo = (wr_t - wr_hi.astype(f32)).astype(bf16)
        hmid, x32, logits_t = _merge(
            h2, row(norm1_gain), a.reshape(n_tok, POOL_W), ao, mo,
            w_pool_proj[l].astype(bf16), w_attn_proj[l].astype(bf16), w_mem_proj[l].astype(bf16),
            w_in_bf, row(b_gate), w_out[l].astype(bf16), row(norm2_gain), wr_hi, wr_lo,
            b_router[l][:, None])

        dest, w_top, cnt = _route(logits_t, tri)
        counts = cnt[:, 0].astype(i32)
        padded = (counts + MOE_BLOCK - 1) // MOE_BLOCK * MOE_BLOCK
        pad_ends = jnp.cumsum(padded)
        blk_start = jnp.arange(n_blocks, dtype=i32) * MOE_BLOCK
        blk_expert = jnp.minimum(jnp.sum((pad_ends[None, :] <= blk_start[:, None]).astype(i32), axis=1),
                                 N_EXP - 1)
        first_blk = (pad_ends - padded) // MOE_BLOCK
        blk_idx = jnp.arange(n_blocks, dtype=i32)
        blk_rows = jnp.clip(counts[blk_expert] - (blk_idx - first_blk[blk_expert]) * MOE_BLOCK,
                           0, MOE_BLOCK).astype(i32)
        dest_flat = dest.reshape(-1)

        xs32 = _dispatch(dest_flat, x32, jnp.zeros((cap * SLAB, LANES), u32), n_tok)
        run_end = jnp.sum((blk_expert[None, :] <= blk_expert[:, None]).astype(i32), axis=1)
        next_expert = jnp.where(run_end < n_blocks, blk_expert[jnp.minimum(run_end, n_blocks - 1)], -1)
        act = _moe_up(blk_expert, blk_rows, next_expert, xs32, w_up[l], b_up[l][:, None, :], sel, cap)
        y32 = _moe_down(blk_expert, blk_rows, next_expert, act, w_down[l], b_down[l][:, None, :], cap)
        h2 = _combine(dest_flat, y32, hmid, w_top.T, n_tok)
    return h2.reshape(bsz, seq, d)
```
